```python
import math
import jax, jax.numpy as jnp
from jax import lax
import numpy as np

D_MODEL = 4096
BATCH = 4
SEQ = 2048
DEPTH = 4
DEC_BATCH = 8
DEC_SEQ = 8
PAST_LEN = 8192
PAGE_SIZE = 128

N_HEADS = 32
HEAD_DIM = D_MODEL // N_HEADS
ATT_DIM = N_HEADS * HEAD_DIM
CONV_W = 3
BLOCK = 256
TOP_K = 3
Q_CHUNK = 8
N_A = DEPTH // 2
N_B = DEPTH - N_A
ALPHA = (2.0 * DEPTH) ** 0.25
BETA = (8.0 * DEPTH) ** -0.25
LN_EPS = 1e-5

kernel_name = 'yoco_shortconv_moba_alibi_deepnorm_step'


def layer_norm(x, g, b):
    xf = x.astype(jnp.float32)
    mu = jnp.mean(xf, axis=-1, keepdims=True)
    var = jnp.mean(jnp.square(xf - mu), axis=-1, keepdims=True)
    return ((xf - mu) * lax.rsqrt(var + LN_EPS) * g + b).astype(x.dtype)


def alibi_slopes(n):
    return 2.0 ** (-8.0 * jnp.arange(1, n + 1, dtype=jnp.float32) / n)


def short_conv_mixer(x, conv_prev, w_in, conv_w, w_out):
    t_len = x.shape[1]
    proj = x @ w_in
    b_g, c_g, h, z = jnp.split(proj, 4, axis=-1)
    u = c_g * h
    ext = jnp.concatenate([conv_prev.astype(u.dtype), u], axis=1)
    conv = conv_w[0] * ext[:, 0:t_len]
    for i in range(1, CONV_W):
        conv = conv + conv_w[i] * ext[:, i:i + t_len]
    y = (b_g * conv) * jax.nn.silu(z)
    return y @ w_out, ext[:, -(CONV_W - 1):]


def moba_attention(q, k, v, q_start, slopes):
    bt, tq, nh, hd = q.shape
    L = k.shape[1]
    nb = -(-L // BLOCK)
    pad = nb * BLOCK - L
    kb = jnp.pad(k, ((0, 0), (0, pad), (0, 0), (0, 0))).reshape(bt, nb, BLOCK, nh, hd).transpose(0, 3, 1, 2, 4)
    vb = jnp.pad(v, ((0, 0), (0, pad), (0, 0), (0, 0))).reshape(bt, nb, BLOCK, nh, hd).transpose(0, 3, 1, 2, 4)
    kmean = jnp.mean(kb.astype(jnp.float32), axis=3)
    k_sel = min(TOP_K, nb)
    qc = math.gcd(tq, Q_CHUNK)
    nc = tq // qc
    q_chunks = q.reshape(bt, nc, qc, nh, hd).transpose(1, 0, 3, 2, 4)
    bi = jnp.arange(bt)[:, None, None, None]
    hi = jnp.arange(nh)[None, :, None, None]
    offs = jnp.arange(BLOCK)
    scale = 1.0 / math.sqrt(hd)

    def one_chunk(args):
        q_c, c = args
        start = q_start + c * qc
        t = start + jnp.arange(qc)
        n_past = t // BLOCK
        own = start // BLOCK
        gate = jnp.einsum('bhqd,bhnd->bhqn', q_c, kmean, preferred_element_type=jnp.float32)
        gate = jnp.where(jnp.arange(nb)[None, :] < n_past[:, None], gate, -jnp.inf)
        _, sel = lax.top_k(gate, k_sel)
        sel_ok = jnp.arange(k_sel)[None, :] < n_past[:, None]
        k_g = kb[bi, hi, sel]
        v_g = vb[bi, hi, sel]
        s_g = jnp.einsum('bhqd,bhqjkd->bhqjk', q_c, k_g, preferred_element_type=jnp.float32) * scale
        dist_g = (t[:, None, None] - (sel[..., None] * BLOCK + offs)).astype(jnp.float32)
        s_g = s_g - slopes[:, None, None, None] * dist_g
        s_g = jnp.where(sel_ok[:, :, None], s_g, -jnp.inf)
        k_o = lax.dynamic_index_in_dim(kb, own, axis=2, keepdims=False)
        v_o = lax.dynamic_index_in_dim(vb, own, axis=2, keepdims=False)
        s_o = jnp.einsum('bhqd,bhkd->bhqk', q_c, k_o, preferred_element_type=jnp.float32) * scale
        dist_o = t[:, None] - (own * BLOCK + offs)[None, :]
        s_o = s_o - slopes[:, None, None] * dist_o.astype(jnp.float32)
        s_o = jnp.where(dist_o >= 0, s_o, -jnp.inf)
        s = jnp.concatenate([s_g.reshape(bt, nh, qc, k_sel * BLOCK), s_o], axis=-1)
        p = jax.nn.softmax(s, axis=-1)
        p_g = p[..., :k_sel * BLOCK].reshape(bt, nh, qc, k_sel, BLOCK)
        p_o = p[..., k_sel * BLOCK:]
        o = (jnp.einsum('bhqjk,bhqjkd->bhqd', p_g, v_g, preferred_element_type=jnp.float32)
             + jnp.einsum('bhqk,bhkd->bhqd', p_o, v_o, preferred_element_type=jnp.float32))
        return o.astype(q_c.dtype)

    out = lax.map(one_chunk, (q_chunks, jnp.arange(nc)))
    return out.transpose(1, 0, 3, 2, 4).reshape(bt, tq, nh, hd)


def moba_mixer(x, k_all, v_all, q_start, w_in, w_out, slopes):
    bt, t_len, _ = x.shape
    proj = x @ w_in
    q = proj[..., :ATT_DIM].reshape(bt, t_len, N_HEADS, HEAD_DIM)
    g = proj[..., ATT_DIM:]
    o = moba_attention(q, k_all, v_all, q_start, slopes).reshape(bt, t_len, ATT_DIM)
    return (o * jax.nn.silu(g)) @ w_out


def trunk(x, conv_prev, k_past, v_past, q_start, w_in_a, conv_w, w_out_a, w_kv, w_in_b, w_out_b, ln_g, ln_b):
    bt, t_len, _ = x.shape
    slopes = alibi_slopes(N_HEADS)
    conv_rows = []
    k_new = v_new = k_all = v_all = None
    for l in range(DEPTH):
        if l < N_A:
            out, rows = short_conv_mixer(x, conv_prev[l], w_in_a[l], conv_w[l], w_out_a[l])
            conv_rows.append(rows)
        else:
            if l == N_A:
                kv = x @ w_kv
                k_new = kv[..., :ATT_DIM].reshape(bt, t_len, N_HEADS, HEAD_DIM)
                v_new = kv[..., ATT_DIM:].reshape(bt, t_len, N_HEADS, HEAD_DIM)
                if k_past is None:
                    k_all, v_all = k_new, v_new
                else:
                    k_all = jnp.concatenate([k_past.astype(k_new.dtype), k_new], axis=1)
                    v_all = jnp.concatenate([v_past.astype(v_new.dtype), v_new], axis=1)
            out = moba_mixer(x, k_all, v_all, q_start, w_in_b[l - N_A], w_out_b[l - N_A], slopes)
        x = layer_norm(ALPHA * x + out, ln_g[l], ln_b[l])
    return x, k_new, v_new, jnp.stack(conv_rows)


def setup_inputs(seed: int = 0) -> dict:
    key = jax.random.key(seed)
    ks = jax.random.split(key, 14)
    n_pages = PAST_LEN // PAGE_SIZE
    n_used = DEC_BATCH * n_pages
    n_phys = n_used + n_used // 4

    def nrm(k, shape, s):
        return jax.random.normal(k, shape, jnp.float32) * s

    x_prompt = nrm(ks[0], (BATCH, SEQ, D_MODEL), 1.0)
    x_sample = nrm(ks[1], (DEC_BATCH, DEC_SEQ, D_MODEL), 1.0)
    cache_k = nrm(ks[2], (n_phys, PAGE_SIZE, N_HEADS, HEAD_DIM), 1.0)
    cache_v = nrm(ks[3], (n_phys, PAGE_SIZE, N_HEADS, HEAD_DIM), 1.0)
    state_conv = nrm(ks[4], (N_A, DEC_BATCH, CONV_W - 1, D_MODEL), 1.0)
    page_table = jax.random.permutation(ks[5], n_phys)[:n_used].reshape(DEC_BATCH, n_pages).astype(jnp.int32)
    w_in_a = nrm(ks[6], (N_A, D_MODEL, 4 * D_MODEL), D_MODEL ** -0.5)
    conv_w = nrm(ks[7], (N_A, CONV_W, D_MODEL), CONV_W ** -0.5)
    w_out_a = nrm(ks[8], (N_A, D_MODEL, D_MODEL), BETA * D_MODEL ** -0.5)
    w_kv = nrm(ks[9], (D_MODEL, 2 * ATT_DIM), D_MODEL ** -0.5)
    w_in_b = nrm(ks[10], (N_B, D_MODEL, 2 * ATT_DIM), D_MODEL ** -0.5)
    w_out_b = nrm(ks[11], (N_B, ATT_DIM, D_MODEL), BETA * ATT_DIM ** -0.5)
    ln_g = 1.0 + nrm(ks[12], (DEPTH, D_MODEL), 0.02)
    ln_b = nrm(ks[13], (DEPTH, D_MODEL), 0.02)
    return {'x_prompt': x_prompt, 'x_sample': x_sample, 'cache_k': cache_k, 'cache_v': cache_v,
            'state_conv': state_conv, 'page_table': page_table, 'w_in_a': w_in_a, 'conv_w': conv_w,
            'w_out_a': w_out_a, 'w_kv': w_kv, 'w_in_b': w_in_b, 'w_out_b': w_out_b,
            'ln_g': ln_g, 'ln_b': ln_b}


def reference(x_prompt, x_sample, cache_k, cache_v, state_conv, page_table, w_in_a, conv_w, w_out_a,
              w_kv, w_in_b, w_out_b, ln_g, ln_b):
    zero_conv = jnp.zeros((N_A, x_prompt.shape[0], CONV_W - 1, D_MODEL), x_prompt.dtype)
    y_prompt, k_prompt, v_prompt, conv_prompt = trunk(
        x_prompt, zero_conv, None, None, 0,
        w_in_a, conv_w, w_out_a, w_kv, w_in_b, w_out_b, ln_g, ln_b)
    db, n_pages = page_table.shape
    past_len = n_pages * PAGE_SIZE
    k_past = cache_k[page_table].reshape(db, past_len, N_HEADS, HEAD_DIM)
    v_past = cache_v[page_table].reshape(db, past_len, N_HEADS, HEAD_DIM)
    y_sample, k_sample, v_sample, conv_sample = trunk(
        x_sample, state_conv, k_past, v_past, past_len,
        w_in_a, conv_w, w_out_a, w_kv, w_in_b, w_out_b, ln_g, ln_b)
    return (y_prompt, y_sample, k_prompt, v_prompt, conv_prompt, k_sample, v_sample, conv_sample)
```

```python
import functools
import math

import numpy as np
import jax
import jax.numpy as jnp
from jax import lax
from jax.experimental import pallas as pl
from jax.experimental.pallas import tpu as pltpu

N_HEADS = 32
HEAD_DIM = 128
CONV_W = 3
BLOCK = 256
TOP_K = 3
PAGE_SIZE = 128
DEPTH = 4
N_A = DEPTH // 2
ALPHA = (2.0 * DEPTH) ** 0.25
LN_EPS = 1e-5
SCALE = 1.0 / math.sqrt(HEAD_DIM)

SUBLANES = 8
LANES = 128
VMEM_LIMIT_BYTES = 56 * 1024 * 1024
MASK_BIG = 2.0 ** 100
PAGES_PER_STEP = 4
LN_ROW_CHUNK = 32
ALIBI_PARTS = 4

F32 = jnp.float32
BF16 = jnp.bfloat16


def _params(*semantics):
    return pltpu.CompilerParams(dimension_semantics=semantics, vmem_limit_bytes=VMEM_LIMIT_BYTES)


def _dot(a, b):
    return jnp.dot(a, b, preferred_element_type=F32)


def _dot_nt(a, b):
    return lax.dot_general(a, b, (((1,), (1,)), ((), ())), preferred_element_type=F32)


def _gated_conv(b_g, u, u_m1, u_m2, z, cw_ref):
    conv = cw_ref[0:1, :] * u_m2 + cw_ref[1:2, :] * u_m1 + cw_ref[2:3, :] * u
    return (b_g * conv) * (z * jax.nn.sigmoid(z))


def _conv_proj_long_kernel(x_ref, wb_ref, wc_ref, wh_ref, wz_ref, cw_ref, prev_ref,
                           y_ref, tail_ref, carry_ref, *, tiles_per_seq):
    m = pl.program_id(1)

    @pl.when(lax.rem(m, tiles_per_seq) == 0)
    def _():
        carry_ref[...] = prev_ref[0]

    x = x_ref[...]
    u = _dot(x, wc_ref[...]) * _dot(x, wh_ref[...])
    tm = u.shape[0]
    row = lax.broadcasted_iota(jnp.int32, u.shape, 0)
    c_m2 = carry_ref[SUBLANES - 2:SUBLANES - 1, :]
    c_m1 = carry_ref[SUBLANES - 1:SUBLANES, :]
    u_m1 = jnp.where(row == 0, c_m1, pltpu.roll(u, 1, 0))
    u_m2 = jnp.where(row == 0, c_m2, jnp.where(row == 1, c_m1, pltpu.roll(u, 2, 0)))
    y = _gated_conv(_dot(x, wb_ref[...]), u, u_m1, u_m2, _dot(x, wz_ref[...]), cw_ref)
    y_ref[...] = y.astype(y_ref.dtype)
    tail = u[tm - SUBLANES:, :]
    carry_ref[...] = tail
    tail_ref[0] = tail


def _conv_proj_short_kernel(x_ref, wb_ref, wc_ref, wh_ref, wz_ref, cw_ref, pm1_ref, pm2_ref,
                            y_ref, u_ref, *, seq_len):
    x = x_ref[...]
    u = _dot(x, wc_ref[...]) * _dot(x, wh_ref[...])
    pos = lax.rem(lax.broadcasted_iota(jnp.int32, u.shape, 0), seq_len)
    u_m1 = jnp.where(pos == 0, pm1_ref[...], pltpu.roll(u, 1, 0))
    u_m2 = jnp.where(pos < 2, pm2_ref[...], pltpu.roll(u, 2, 0))
    y = _gated_conv(_dot(x, wb_ref[...]), u, u_m1, u_m2, _dot(x, wz_ref[...]), cw_ref)
    y_ref[...] = y.astype(y_ref.dtype)
    u_ref[...] = u


def _conv_proj(x16, w_in16, conv_w, conv_prev, seq_len):
    m_rows, d = x16.shape
    n_seq = m_rows // seq_len
    tn = 256
    nj = d // tn
    w_specs = [pl.BlockSpec((d, tn), functools.partial(lambda j, m, g: (0, g * nj + j), g=g))
               for g in range(4)]
    cw_spec = pl.BlockSpec((CONV_W, tn), lambda j, m: (0, j))
    if seq_len >= 512:
        tm = 1024 if seq_len % 1024 == 0 else 512
        tiles_per_seq = seq_len // tm
        prev8 = jnp.pad(conv_prev.astype(F32), ((0, 0), (SUBLANES - (CONV_W - 1), 0), (0, 0)))
        y, tail = pl.pallas_call(
            functools.partial(_conv_proj_long_kernel, tiles_per_seq=tiles_per_seq),
            out_shape=(jax.ShapeDtypeStruct((m_rows, d), BF16),
                       jax.ShapeDtypeStruct((n_seq, SUBLANES, d), F32)),
            grid=(nj, m_rows // tm),
            in_specs=[pl.BlockSpec((tm, d), lambda j, m: (m, 0))] + w_specs + [
                cw_spec,
                pl.BlockSpec((1, SUBLANES, tn), lambda j, m: (m // tiles_per_seq, 0, j))],
            out_specs=(pl.BlockSpec((tm, tn), lambda j, m: (m, j)),
                       pl.BlockSpec((1, SUBLANES, tn), lambda j, m: (m // tiles_per_seq, 0, j))),
            scratch_shapes=[pltpu.VMEM((SUBLANES, tn), F32)],
            compiler_params=_params("arbitrary", "arbitrary"),
            name="conv_proj_long",
        )(x16, w_in16, w_in16, w_in16, w_in16, conv_w, prev8)
        return y, tail[:, SUBLANES - (CONV_W - 1):, :]

    prev = conv_prev.astype(F32)
    zeros = jnp.zeros((n_seq, seq_len - 1, d), F32)
    pm1 = jnp.concatenate([prev[:, 1:2], zeros], axis=1).reshape(m_rows, d)
    pm2 = jnp.concatenate([prev, zeros[:, 1:]], axis=1).reshape(m_rows, d)
    y, u = pl.pallas_call(
        functools.partial(_conv_proj_short_kernel, seq_len=seq_len),
        out_shape=(jax.ShapeDtypeStruct((m_rows, d), BF16),
                   jax.ShapeDtypeStruct((m_rows, d), F32)),
        grid=(nj, 1),
        in_specs=[pl.BlockSpec((m_rows, d), lambda j, m: (0, 0))] + w_specs + [
            cw_spec,
            pl.BlockSpec((m_rows, tn), lambda j, m: (0, j)),
            pl.BlockSpec((m_rows, tn), lambda j, m: (0, j))],
        out_specs=(pl.BlockSpec((m_rows, tn), lambda j, m: (0, j)),
                   pl.BlockSpec((m_rows, tn), lambda j, m: (0, j))),
        compiler_params=_params("arbitrary", "arbitrary"),
        name="conv_proj_short",
    )(x16, w_in16, w_in16, w_in16, w_in16, conv_w, pm1, pm2)
    return y, u.reshape(n_seq, seq_len, d)[:, seq_len - (CONV_W - 1):, :]


def _out_proj_ln_kernel(y_ref, w_ref, xres_ref, g_ref, b_ref, xo_ref, xo16_ref, *, tn):
    j = pl.program_id(1)
    r = ALPHA * xres_ref[...] + _dot(y_ref[...], w_ref[...])
    xo_ref[:, pl.ds(pl.multiple_of(j * tn, tn), tn)] = r

    @pl.when(j == pl.num_programs(1) - 1)
    def _():
        gain = g_ref[...]
        bias = b_ref[...]

        def body(i, carry):
            rows = pl.ds(pl.multiple_of(i * LN_ROW_CHUNK, LN_ROW_CHUNK), LN_ROW_CHUNK)
            v = xo_ref[rows, :]
            mu = jnp.mean(v, axis=-1, keepdims=True)
            dlt = v - mu
            var = jnp.mean(dlt * dlt, axis=-1, keepdims=True)
            o = dlt * lax.rsqrt(var + LN_EPS) * gain + bias
            xo_ref[rows, :] = o
            xo16_ref[rows, :] = o.astype(xo16_ref.dtype)
            return carry

        lax.fori_loop(0, xo_ref.shape[0] // LN_ROW_CHUNK, body, 0)


def _out_proj_ln(y16, w16, x_res, gain, bias):
    m_rows, k = y16.shape
    d = w16.shape[1]
    tm = min(512, m_rows)
    tn = 512
    assert m_rows % tm == 0 and tm % LN_ROW_CHUNK == 0 and d % tn == 0
    return pl.pallas_call(
        functools.partial(_out_proj_ln_kernel, tn=tn),
        out_shape=(jax.ShapeDtypeStruct((m_rows, d), F32), jax.ShapeDtypeStruct((m_rows, d), BF16)),
        grid=(m_rows // tm, d // tn),
        in_specs=[pl.BlockSpec((tm, k), lambda m, j: (m, 0)),
                  pl.BlockSpec((k, tn), lambda m, j: (0, j)),
                  pl.BlockSpec((tm, tn), lambda m, j: (m, j)),
                  pl.BlockSpec((1, d), lambda m, j: (0, 0)),
                  pl.BlockSpec((1, d), lambda m, j: (0, 0))],
        out_specs=(pl.BlockSpec((tm, d), lambda m, j: (m, 0)),
                   pl.BlockSpec((tm, d), lambda m, j: (m, 0))),
        compiler_params=_params("arbitrary", "arbitrary"),
        name="out_proj_ln",
    )(y16, w16, x_res, gain.reshape(1, d), bias.reshape(1, d))


def _dual_proj_kernel(x_ref, wa_ref, wb_ref, oa_ref, ob_ref):
    x = x_ref[...]
    oa_ref[...] = _dot(x, wa_ref[...])
    ob_ref[...] = _dot(x, wb_ref[...])


def _dual_proj(x16, w16):
    m_rows, k = x16.shape
    n_half = w16.shape[1] // 2
    tm = next(t for t in (1024, 512, m_rows) if m_rows % t == 0)
    tn = 512
    nj = n_half // tn
    out = jax.ShapeDtypeStruct((m_rows, n_half), F32)
    return pl.pallas_call(
        _dual_proj_kernel,
        out_shape=(out, out),
        grid=(m_rows // tm, nj),
        in_specs=[pl.BlockSpec((tm, k), lambda m, j: (m, 0)),
                  pl.BlockSpec((k, tn), lambda m, j: (0, j)),
                  pl.BlockSpec((k, tn), lambda m, j: (0, nj + j))],
        out_specs=(pl.BlockSpec((tm, tn), lambda m, j: (m, j)),
                   pl.BlockSpec((tm, tn), lambda m, j: (m, j))),
        compiler_params=_params("arbitrary", "arbitrary"),
        name="dual_proj",
    )(x16, w16, w16)


def _moba_prefill_kernel(q_ref, k_ref, v_ref, g_ref, kx_ref, qx_ref, o_ref, qa_ref, ka_ref, vb_ref):
    seq = q_ref.shape[0]
    nb = seq // BLOCK
    q16 = q_ref[...].astype(BF16)
    k = k_ref[...]

    kmean = jnp.mean(k.reshape(nb, BLOCK, HEAD_DIM), axis=1)
    gate_t = _dot_nt(kmean.astype(BF16), q16)
    blk = lax.broadcasted_iota(jnp.int32, (nb, seq), 0)
    q_blk = lax.broadcasted_iota(jnp.int32, (nb, seq), 1) // BLOCK
    rank = jnp.zeros((nb, seq), jnp.int32)
    for other in range(nb):
        g_o = gate_t[other:other + 1, :]
        beats = (g_o > gate_t) | ((g_o == gate_t) & (other < blk))
        rank += jnp.where((q_blk > other) & beats, 1, 0)
    dropped = ((blk < q_blk) & (rank >= TOP_K)).astype(F32)
    lane = lax.broadcasted_iota(jnp.int32, (nb, LANES), 1)
    spread = jnp.where(lane == lax.broadcasted_iota(jnp.int32, (nb, LANES), 0), -MASK_BIG, 0.0)
    sel_bias = lax.dot_general(dropped, spread, (((0,), (0,)), ((), ())),
                               preferred_element_type=F32)

    qa_ref[:, :HEAD_DIM] = q16
    qa_ref[:, HEAD_DIM:] = (sel_bias + qx_ref[0, 0:1, :]).astype(BF16)
    ka_ref[:, :HEAD_DIM] = k.astype(BF16)
    ka_ref[:, HEAD_DIM:] = kx_ref[...]
    vb_ref[...] = v_ref[...].astype(BF16)

    r_i = lax.broadcasted_iota(jnp.int32, (BLOCK, BLOCK), 0)
    c_i = lax.broadcasted_iota(jnp.int32, (BLOCK, BLOCK), 1)
    causal = jnp.where(c_i <= r_i, 0.0, -jnp.inf)
    for i in range(nb):
        rows = slice(i * BLOCK, (i + 1) * BLOCK)
        n_keys = (i + 1) * BLOCK
        s = _dot_nt(qa_ref[rows, :], ka_ref[0:n_keys, :])
        s_own = s[:, i * BLOCK:] + causal
        s = s_own if i == 0 else jnp.concatenate([s[:, :i * BLOCK], s_own], axis=1)
        m = jnp.max(s, axis=-1, keepdims=True)
        p = jnp.exp((s - m) * SCALE)
        l = jnp.sum(p, axis=-1, keepdims=True)
        o = _dot(p.astype(BF16), vb_ref[0:n_keys, :]) / l
        gate = g_ref[rows, :]
        o_ref[rows, :] = (o * (gate * jax.nn.sigmoid(gate))).astype(o_ref.dtype)


def _alibi_slopes():
    return 2.0 ** (-8.0 * jnp.arange(1, N_HEADS + 1, dtype=F32) / N_HEADS)


def _moba_prefill(q, k, v, g, n_batch, seq):
    nb = seq // BLOCK
    assert nb <= SUBLANES and seq % BLOCK == 0
    pos = np.arange(seq)
    kx = np.zeros((seq, LANES), np.float32)
    kx[pos, pos // BLOCK] = 1.0
    kx[:, SUBLANES:SUBLANES + ALIBI_PARTS] = (pos % BLOCK)[:, None]
    kx[:, SUBLANES + ALIBI_PARTS:SUBLANES + 2 * ALIBI_PARTS] = ((pos // BLOCK) * BLOCK)[:, None]
    rest = _alibi_slopes() / SCALE
    pieces = []
    for _ in range(ALIBI_PARTS):
        piece = rest.astype(BF16).astype(F32)
        pieces.append(piece)
        rest = rest - piece
    pieces = jnp.stack(pieces, axis=1)
    qx = jnp.zeros((N_HEADS, LANES), F32)
    qx = qx.at[:, SUBLANES:SUBLANES + ALIBI_PARTS].set(pieces)
    qx = qx.at[:, SUBLANES + ALIBI_PARTS:SUBLANES + 2 * ALIBI_PARTS].set(pieces)
    qx = jnp.broadcast_to(qx[:, None, :], (N_HEADS, SUBLANES, LANES))
    head_spec = pl.BlockSpec((seq, HEAD_DIM), lambda b, h: (b, h))
    return pl.pallas_call(
        _moba_prefill_kernel,
        out_shape=jax.ShapeDtypeStruct(q.shape, BF16),
        grid=(n_batch, N_HEADS),
        in_specs=[head_spec, head_spec, head_spec, head_spec,
                  pl.BlockSpec((seq, LANES), lambda b, h: (0, 0)),
                  pl.BlockSpec((1, SUBLANES, LANES), lambda b, h: (h, 0, 0))],
        out_specs=head_spec,
        scratch_shapes=[pltpu.VMEM((seq, 2 * HEAD_DIM), BF16),
                        pltpu.VMEM((seq, 2 * HEAD_DIM), BF16),
                        pltpu.VMEM((seq, HEAD_DIM), BF16)],
        compiler_params=_params("arbitrary", "arbitrary"),
        name="moba_prefill",
    )(q, k, v, g, jnp.asarray(kx, BF16), qx)


def _block_diag_rows(x, n_q):
    rows = N_HEADS * n_q
    tiled = jnp.concatenate([x] * N_HEADS, axis=0)
    r_h = lax.broadcasted_iota(jnp.int32, (rows, x.shape[1]), 0) // n_q
    c_h = lax.broadcasted_iota(jnp.int32, (rows, x.shape[1]), 1) // HEAD_DIM
    return jnp.where(r_h == c_h, tiled, 0.0)


def _pad_rows(x, rows):
    return jnp.concatenate([x, jnp.zeros((rows - x.shape[0], x.shape[1]), x.dtype)], axis=0)


def _moba_decode_scores_kernel(pt_ref, q_ref, knew_ref, slope_ref, *rest, n_q, past_len):
    pages = rest[:PAGES_PER_STEP]
    p_ref, pown_ref, l_ref, s_ref, ksum_ref, kc_ref, qbd_ref = rest[PAGES_PER_STEP:]
    c = pl.program_id(1)
    n_chunks = pl.num_programs(1)
    chunk = PAGES_PER_STEP * PAGE_SIZE
    blocks_per_chunk = chunk // BLOCK
    n_blocks = past_len // BLOCK
    rows = N_HEADS * n_q

    @pl.when(c == 0)
    def _():
        qbd_ref[...] = _block_diag_rows(q_ref[0], n_q).astype(BF16)

    sums = []
    for i, page in enumerate(pages):
        kp = page[0]
        kc_ref[i * PAGE_SIZE:(i + 1) * PAGE_SIZE, :] = kp.astype(BF16)
        sums.append(jnp.sum(kp, axis=0, keepdims=True))
    pages_per_block = BLOCK // PAGE_SIZE
    for bi in range(blocks_per_chunk):
        tot = sums[bi * pages_per_block]
        for extra in sums[bi * pages_per_block + 1:(bi + 1) * pages_per_block]:
            tot = tot + extra
        ksum_ref[pl.ds(c * blocks_per_chunk + bi, 1), :] = tot
    s_ref[c] = _dot_nt(qbd_ref[...], kc_ref[...])

    @pl.when(c == n_chunks - 1)
    def _():
        qbd = qbd_ref[...]
        kmean = (ksum_ref[...] / BLOCK).astype(BF16)
        gate = _dot_nt(qbd, kmean)
        idx = lax.broadcasted_iota(jnp.int32, gate.shape, 1)
        picked = jnp.zeros(gate.shape, jnp.bool_)
        for _ in range(TOP_K):
            best = jnp.max(gate, axis=-1, keepdims=True)
            first = jnp.min(jnp.where(gate == best, idx, n_blocks), axis=-1, keepdims=True)
            hit = idx == first
            picked = picked | hit
            gate = jnp.where(hit, -jnp.inf, gate)
        picked16 = picked.astype(BF16)

        slope = slope_ref[:, 0:1]
        q_i = lax.rem(lax.broadcasted_iota(jnp.int32, (rows, 1), 0), n_q)
        t = (past_len + q_i).astype(F32)

        def mask_chunk(ci, m_run):
            blk_of_col = ci * blocks_per_chunk + lax.broadcasted_iota(
                jnp.int32, (n_blocks, chunk), 1) // BLOCK
            expand = (lax.broadcasted_iota(jnp.int32, (n_blocks, chunk), 0) == blk_of_col).astype(BF16)
            keep = _dot(picked16, expand)
            pos = (ci * chunk + lax.broadcasted_iota(jnp.int32, (1, chunk), 1)).astype(F32)
            s = s_ref[ci] * SCALE - slope * (t - pos)
            s = jnp.where(keep > 0.5, s, -jnp.inf)
            s_ref[ci] = s
            return jnp.maximum(m_run, jnp.max(s, axis=-1, keepdims=True))

        m_past = lax.fori_loop(0, n_chunks, mask_chunk, jnp.full((rows, 1), -jnp.inf, F32))

        k_own = _pad_rows(knew_ref[0], LANES).astype(BF16)
        j_own = lax.broadcasted_iota(jnp.int32, (rows, LANES), 1)
        s_own = _dot_nt(qbd, k_own) * SCALE - slope * (q_i - j_own).astype(F32)
        s_own = jnp.where(j_own <= q_i, s_own, -jnp.inf)
        m_all = jnp.maximum(m_past, jnp.max(s_own, axis=-1, keepdims=True))
        p_own = jnp.exp(s_own - m_all)
        pown_ref[0] = p_own

        def exp_chunk(ci, l_run):
            p = jnp.exp(s_ref[ci] - m_all)
            p_ref[0, ci] = p.astype(p_ref.dtype)
            return l_run + jnp.sum(p, axis=-1, keepdims=True)

        l_all = lax.fori_loop(0, n_chunks, exp_chunk, jnp.sum(p_own, axis=-1, keepdims=True))
        l_ref[0] = jnp.broadcast_to(l_all, (rows, LANES))


def _moba_decode_values_kernel(pt_ref, p_ref, pown_ref, l_ref, vnew_ref, g_ref, *rest, n_q):
    pages = rest[:PAGES_PER_STEP]
    o_ref, acc_ref, vc_ref = rest[PAGES_PER_STEP:]
    c = pl.program_id(1)
    rows = N_HEADS * n_q

    for i, page in enumerate(pages):
        vc_ref[i * PAGE_SIZE:(i + 1) * PAGE_SIZE, :] = page[0].astype(BF16)
    part = _dot(p_ref[0, 0], vc_ref[...])

    @pl.when(c == 0)
    def _():
        acc_ref[...] = part

    @pl.when(c > 0)
    def _():
        acc_ref[...] += part

    @pl.when(c == pl.num_programs(1) - 1)
    def _():
        v_own = _pad_rows(vnew_ref[0], LANES).astype(BF16)
        full = (acc_ref[...] + _dot(pown_ref[0].astype(BF16), v_own)) / l_ref[0][:, 0:1]
        r_h = lax.broadcasted_iota(jnp.int32, full.shape, 0) // n_q
        c_h = lax.broadcasted_iota(jnp.int32, full.shape, 1) // HEAD_DIM
        o = jnp.sum(jnp.where(r_h == c_h, full, 0.0).reshape(N_HEADS, n_q, full.shape[1]), axis=0)
        gate = g_ref[0]
        o_ref[0] = o * (gate * jax.nn.sigmoid(gate))


def _moba_decode(q, g, k_new, v_new, cache_k, cache_v, page_table, n_batch, n_q):
    d = q.shape[1]
    n_pages = page_table.shape[1]
    past_len = n_pages * PAGE_SIZE
    assert n_q == SUBLANES and past_len % BLOCK == 0 and n_pages % PAGES_PER_STEP == 0
    n_chunks = n_pages // PAGES_PER_STEP
    chunk = PAGES_PER_STEP * PAGE_SIZE
    n_blocks = past_len // BLOCK
    rows = N_HEADS * n_q
    pt = page_table.reshape(-1).astype(jnp.int32)
    ck = cache_k.reshape(cache_k.shape[0], PAGE_SIZE, d)
    cv = cache_v.reshape(cache_v.shape[0], PAGE_SIZE, d)
    q3, g3 = q.reshape(n_batch, n_q, d), g.reshape(n_batch, n_q, d)
    kn3, vn3 = k_new.reshape(n_batch, n_q, d), v_new.reshape(n_batch, n_q, d)
    slope_rows = jnp.broadcast_to(jnp.repeat(_alibi_slopes(), n_q)[:, None], (rows, LANES))

    def page_spec(i):
        return pl.BlockSpec(
            (1, PAGE_SIZE, d),
            lambda b, c, pt_ref: (pt_ref[b * n_pages + c * PAGES_PER_STEP + i], 0, 0))

    tok_spec = pl.BlockSpec((1, n_q, d), lambda b, c, pt_ref: (b, 0, 0))
    row_spec = pl.BlockSpec((1, rows, LANES), lambda b, c, pt_ref: (b, 0, 0))
    probs, p_own, denom = pl.pallas_call(
        functools.partial(_moba_decode_scores_kernel, n_q=n_q, past_len=past_len),
        out_shape=(jax.ShapeDtypeStruct((n_batch, n_chunks, rows, chunk), BF16),
                   jax.ShapeDtypeStruct((n_batch, rows, LANES), F32),
                   jax.ShapeDtypeStruct((n_batch, rows, LANES), F32)),
        grid_spec=pltpu.PrefetchScalarGridSpec(
            num_scalar_prefetch=1,
            grid=(n_batch, n_chunks),
            in_specs=[tok_spec, tok_spec,
                      pl.BlockSpec((rows, LANES), lambda b, c, pt_ref: (0, 0))]
                     + [page_spec(i) for i in range(PAGES_PER_STEP)],
            out_specs=(pl.BlockSpec((1, n_chunks, rows, chunk), lambda b, c, pt_ref: (b, 0, 0, 0)),
                       row_spec, row_spec),
            scratch_shapes=[pltpu.VMEM((n_chunks, rows, chunk), F32),
                            pltpu.VMEM((n_blocks, d), F32),
                            pltpu.VMEM((chunk, d), BF16),
                            pltpu.VMEM((rows, d), BF16)]),
        compiler_params=_params("arbitrary", "arbitrary"),
        name="moba_decode_scores",
    )(pt, q3, kn3, slope_rows, *([ck] * PAGES_PER_STEP))

    out = pl.pallas_call(
        functools.partial(_moba_decode_values_kernel, n_q=n_q),
        out_shape=jax.ShapeDtypeStruct((n_batch, n_q, d), F32),
        grid_spec=pltpu.PrefetchScalarGridSpec(
            num_scalar_prefetch=1,
            grid=(n_batch, n_chunks),
            in_specs=[pl.BlockSpec((1, 1, rows, chunk), lambda b, c, pt_ref: (b, c, 0, 0)),
                      row_spec, row_spec, tok_spec, tok_spec]
                     + [page_spec(i) for i in range(PAGES_PER_STEP)],
            out_specs=tok_spec,
            scratch_shapes=[pltpu.VMEM((rows, d), F32),
                            pltpu.VMEM((chunk, d), BF16)]),
        compiler_params=_params("arbitrary", "arbitrary"),
        name="moba_decode_values",
    )(pt, probs, p_own, denom, vn3, g3, *([cv] * PAGES_PER_STEP))
    return out.reshape(n_batch * n_q, d).astype(BF16)


def _trunk(x, conv_prev, weights, attention):
    w_in_a, conv_w, w_out_a, w_kv, w_in_b, w_out_b, ln_g, ln_b = weights
    n_seq, t_len, d = x.shape
    x32 = x.reshape(n_seq * t_len, d)
    x16 = x32.astype(BF16)
    conv_rows = []
    for l in range(N_A):
        y16, rows = _conv_proj(x16, w_in_a[l], conv_w[l], conv_prev[l], t_len)
        conv_rows.append(rows)
        x32, x16 = _out_proj_ln(y16, w_out_a[l], x32, ln_g[l], ln_b[l])
    k_new, v_new = _dual_proj(x16, w_kv)
    for l in range(N_A, DEPTH):
        q, g = _dual_proj(x16, w_in_b[l - N_A])
        o16 = attention(q, g, k_new, v_new)
        x32, x16 = _out_proj_ln(o16, w_out_b[l - N_A], x32, ln_g[l], ln_b[l])
    heads = (n_seq, t_len, N_HEADS, HEAD_DIM)
    return (x32.reshape(n_seq, t_len, d), k_new.reshape(heads), v_new.reshape(heads),
            jnp.stack(conv_rows))


def kernel(x_prompt, x_sample, cache_k, cache_v, state_conv, page_table, w_in_a, conv_w, w_out_a,
           w_kv, w_in_b, w_out_b, ln_g, ln_b):
    weights = (w_in_a.astype(BF16), conv_w, w_out_a.astype(BF16), w_kv.astype(BF16),
               w_in_b.astype(BF16), w_out_b.astype(BF16), ln_g, ln_b)
    n_prompt, seq, d = x_prompt.shape
    n_dec, dec_seq, _ = x_sample.shape

    zero_conv = jnp.zeros((N_A, n_prompt, CONV_W - 1, d), x_prompt.dtype)
    y_prompt, k_prompt, v_prompt, conv_prompt = _trunk(
        x_prompt, zero_conv, weights,
        lambda q, g, k, v: _moba_prefill(q, k, v, g, n_prompt, seq))
    y_sample, k_sample, v_sample, conv_sample = _trunk(
        x_sample, state_conv, weights,
        lambda q, g, k, v: _moba_decode(q, g, k, v, cache_k, cache_v, page_table, n_dec, dec_seq))
    return (y_prompt, y_sample, k_prompt, v_prompt, conv_prompt, k_sample, v_sample, conv_sample)
```

```python
import functools
import math

import numpy as np
import jax
import jax.numpy as jnp
from jax import lax
from jax.experimental import pallas as pl
from jax.experimental.pallas import tpu as pltpu

N_HEADS = 32
HEAD_DIM = 128
CONV_W = 3
BLOCK = 256
TOP_K = 3
PAGE_SIZE = 128
DEPTH = 4
N_A = DEPTH // 2
ALPHA = (2.0 * DEPTH) ** 0.25
LN_EPS = 1e-5
SCALE = 1.0 / math.sqrt(HEAD_DIM)
LOG2_E = math.log2(math.e)

SUBLANES = 8
LANES = 128
VMEM_LIMIT_BYTES = 56 * 1024 * 1024
MASK_BIG = 2.0 ** 100
PAGES_PER_STEP = 4
HEAD_GROUPS = N_HEADS // SUBLANES
ALIBI_PARTS = 4

F32 = jnp.float32
BF16 = jnp.bfloat16


def _params(*semantics):
    return pltpu.CompilerParams(dimension_semantics=semantics, vmem_limit_bytes=VMEM_LIMIT_BYTES)


def _dot(a, b):
    return jnp.dot(a, b, preferred_element_type=F32)


def _dot_nt(a, b):
    return lax.dot_general(a, b, (((1,), (1,)), ((), ())), preferred_element_type=F32)


def _gated_conv(b_g, u, u_m1, u_m2, z, cw_ref):
    conv = cw_ref[0:1, :] * u_m2 + cw_ref[1:2, :] * u_m1 + cw_ref[2:3, :] * u
    return (b_g * conv) * (z * jax.nn.sigmoid(z))


def _conv_proj_long_kernel(x_ref, wb_ref, wc_ref, wh_ref, wz_ref, cw_ref, prev_ref,
                           y_ref, tail_ref, carry_ref, *, tiles_per_seq):
    m = pl.program_id(1)

    @pl.when(lax.rem(m, tiles_per_seq) == 0)
    def _():
        carry_ref[...] = prev_ref[0]

    x = x_ref[...]
    u = _dot(x, wc_ref[...]) * _dot(x, wh_ref[...])
    tm = u.shape[0]
    row = lax.broadcasted_iota(jnp.int32, u.shape, 0)
    c_m2 = carry_ref[SUBLANES - 2:SUBLANES - 1, :]
    c_m1 = carry_ref[SUBLANES - 1:SUBLANES, :]
    u_m1 = jnp.where(row == 0, c_m1, pltpu.roll(u, 1, 0))
    u_m2 = jnp.where(row == 0, c_m2, jnp.where(row == 1, c_m1, pltpu.roll(u, 2, 0)))
    y = _gated_conv(_dot(x, wb_ref[...]), u, u_m1, u_m2, _dot(x, wz_ref[...]), cw_ref)
    y_ref[...] = y.astype(y_ref.dtype)
    tail = u[tm - SUBLANES:, :]
    carry_ref[...] = tail
    tail_ref[0] = tail


def _conv_proj_short_kernel(x_ref, wb_ref, wc_ref, wh_ref, wz_ref, cw_ref, pm1_ref, pm2_ref,
                            y_ref, u_ref, *, seq_len):
    x = x_ref[...]
    u = _dot(x, wc_ref[...]) * _dot(x, wh_ref[...])
    pos = lax.rem(lax.broadcasted_iota(jnp.int32, u.shape, 0), seq_len)
    u_m1 = jnp.where(pos == 0, pm1_ref[...], pltpu.roll(u, 1, 0))
    u_m2 = jnp.where(pos < 2, pm2_ref[...], pltpu.roll(u, 2, 0))
    y = _gated_conv(_dot(x, wb_ref[...]), u, u_m1, u_m2, _dot(x, wz_ref[...]), cw_ref)
    y_ref[...] = y.astype(y_ref.dtype)
    u_ref[...] = u


def _conv_proj(x16, w_in16, layer, conv_w, conv_prev, seq_len):
    m_rows, d = x16.shape
    n_seq = m_rows // seq_len
    tn = 256
    nj = d // tn
    w_specs = [pl.BlockSpec((None, d, tn),
                            functools.partial(lambda j, m, g: (layer, 0, g * nj + j), g=g))
               for g in range(4)]
    cw_spec = pl.BlockSpec((CONV_W, tn), lambda j, m: (0, j))
    if seq_len >= 512:
        tm = 1024 if seq_len % 1024 == 0 else 512
        tiles_per_seq = seq_len // tm
        prev8 = jnp.pad(conv_prev.astype(F32), ((0, 0), (SUBLANES - (CONV_W - 1), 0), (0, 0)))
        y, tail = pl.pallas_call(
            functools.partial(_conv_proj_long_kernel, tiles_per_seq=tiles_per_seq),
            out_shape=(jax.ShapeDtypeStruct((m_rows, d), BF16),
                       jax.ShapeDtypeStruct((n_seq, SUBLANES, d), F32)),
            grid=(nj, m_rows // tm),
            in_specs=[pl.BlockSpec((tm, d), lambda j, m: (m, 0))] + w_specs + [
                cw_spec,
                pl.BlockSpec((1, SUBLANES, tn), lambda j, m: (m // tiles_per_seq, 0, j))],
            out_specs=(pl.BlockSpec((tm, tn), lambda j, m: (m, j)),
                       pl.BlockSpec((1, SUBLANES, tn), lambda j, m: (m // tiles_per_seq, 0, j))),
            scratch_shapes=[pltpu.VMEM((SUBLANES, tn), F32)],
            compiler_params=_params("arbitrary", "arbitrary"),
            name="conv_proj_long",
        )(x16, w_in16, w_in16, w_in16, w_in16, conv_w, prev8)
        return y, tail[:, SUBLANES - (CONV_W - 1):, :]

    prev = conv_prev.astype(F32)
    zeros = jnp.zeros((n_seq, seq_len - 1, d), F32)
    pm1 = jnp.concatenate([prev[:, 1:2], zeros], axis=1).reshape(m_rows, d)
    pm2 = jnp.concatenate([prev, zeros[:, 1:]], axis=1).reshape(m_rows, d)
    y, u = pl.pallas_call(
        functools.partial(_conv_proj_short_kernel, seq_len=seq_len),
        out_shape=(jax.ShapeDtypeStruct((m_rows, d), BF16),
                   jax.ShapeDtypeStruct((m_rows, d), F32)),
        grid=(nj, 1),
        in_specs=[pl.BlockSpec((m_rows, d), lambda j, m: (0, 0))] + w_specs + [
            cw_spec,
            pl.BlockSpec((m_rows, tn), lambda j, m: (0, j)),
            pl.BlockSpec((m_rows, tn), lambda j, m: (0, j))],
        out_specs=(pl.BlockSpec((m_rows, tn), lambda j, m: (0, j)),
                   pl.BlockSpec((m_rows, tn), lambda j, m: (0, j))),
        compiler_params=_params("arbitrary", "arbitrary"),
        name="conv_proj_short",
    )(x16, w_in16, w_in16, w_in16, w_in16, conv_w, pm1, pm2)
    return y, u.reshape(n_seq, seq_len, d)[:, seq_len - (CONV_W - 1):, :]


def _fold_lanes(v):
    out = v[:, :LANES]
    for t in range(1, v.shape[1] // LANES):
        out = out + v[:, t * LANES:(t + 1) * LANES]
    return out


def _out_proj_ln_kernel(y_ref, w_ref, xres_ref, g_ref, b_ref, xo_ref, xo16_ref,
                        r_ref, shift_ref, sum_ref, sq_ref, *, nj, d):
    j = pl.program_id(1)

    @pl.when(j < nj)
    def _():
        r = ALPHA * xres_ref[...] + _dot(y_ref[...], w_ref[...])
        r_ref[j] = r

        @pl.when(j == 0)
        def _():
            shift_ref[...] = jnp.broadcast_to(jnp.mean(r, axis=-1, keepdims=True), shift_ref.shape)
            sum_ref[...] = jnp.zeros_like(sum_ref)
            sq_ref[...] = jnp.zeros_like(sq_ref)

        dlt = r - shift_ref[:, 0:1]
        sum_ref[...] += _fold_lanes(dlt)
        sq_ref[...] += _fold_lanes(dlt * dlt)

    @pl.when(j == nj)
    def _():
        e1 = jnp.sum(sum_ref[...], axis=-1, keepdims=True) / d
        e2 = jnp.sum(sq_ref[...], axis=-1, keepdims=True) / d
        var = jnp.maximum(e2 - e1 * e1, 0.0)
        shift_ref[...] = jnp.broadcast_to(shift_ref[:, 0:1] + e1, shift_ref.shape)
        sum_ref[...] = jnp.broadcast_to(lax.rsqrt(var + LN_EPS), sum_ref.shape)

    @pl.when(j >= nj)
    def _():
        o = (r_ref[j - nj] - shift_ref[:, 0:1]) * sum_ref[:, 0:1] * g_ref[...] + b_ref[...]
        xo_ref[...] = o
        xo16_ref[...] = o.astype(xo16_ref.dtype)


def _out_proj_ln(y16, w16, layer, x_res, gain, bias):
    m_rows, k = y16.shape
    d = w16.shape[2]
    tm = min(1024, m_rows)
    tn = 512
    nj = d // tn
    assert m_rows % tm == 0 and d % tn == 0
    col_in = lambda m, j: (m, jnp.minimum(j, nj - 1))
    col_out = lambda m, j: (m, jnp.maximum(j - nj, 0))
    vec_out = lambda m, j: (0, jnp.maximum(j - nj, 0))
    return pl.pallas_call(
        functools.partial(_out_proj_ln_kernel, nj=nj, d=d),
        out_shape=(jax.ShapeDtypeStruct((m_rows, d), F32), jax.ShapeDtypeStruct((m_rows, d), BF16)),
        grid=(m_rows // tm, 2 * nj),
        in_specs=[pl.BlockSpec((tm, k), lambda m, j: (m, 0)),
                  pl.BlockSpec((None, k, tn), lambda m, j: (layer, 0, jnp.minimum(j, nj - 1))),
                  pl.BlockSpec((tm, tn), col_in),
                  pl.BlockSpec((1, tn), vec_out),
                  pl.BlockSpec((1, tn), vec_out)],
        out_specs=(pl.BlockSpec((tm, tn), col_out), pl.BlockSpec((tm, tn), col_out)),
        scratch_shapes=[pltpu.VMEM((nj, tm, tn), F32),
                        pltpu.VMEM((tm, LANES), F32),
                        pltpu.VMEM((tm, LANES), F32),
                        pltpu.VMEM((tm, LANES), F32)],
        compiler_params=_params("arbitrary", "arbitrary"),
        name="out_proj_ln",
    )(y16, w16, x_res, gain.reshape(1, d), bias.reshape(1, d))


def _dual_proj_kernel(x_ref, wa_ref, wb_ref, oa_ref, ob_ref):
    x = x_ref[...]
    oa_ref[...] = _dot(x, wa_ref[...])
    ob_ref[...] = _dot(x, wb_ref[...])


def _dual_proj(x16, w16, layer):
    m_rows, k = x16.shape
    n_half = w16.shape[2] // 2
    tm = next(t for t in (1024, 512, m_rows) if m_rows % t == 0)
    tn = 512
    nj = n_half // tn
    out = jax.ShapeDtypeStruct((m_rows, n_half), F32)
    return pl.pallas_call(
        _dual_proj_kernel,
        out_shape=(out, out),
        grid=(m_rows // tm, nj),
        in_specs=[pl.BlockSpec((tm, k), lambda m, j: (m, 0)),
                  pl.BlockSpec((None, k, tn), lambda m, j: (layer, 0, j)),
                  pl.BlockSpec((None, k, tn), lambda m, j: (layer, 0, nj + j))],
        out_specs=(pl.BlockSpec((tm, tn), lambda m, j: (m, j)),
                   pl.BlockSpec((tm, tn), lambda m, j: (m, j))),
        compiler_params=_params("arbitrary", "arbitrary"),
        name="dual_proj",
    )(x16, w16, w16)


def _moba_prefill_kernel(q_ref, k_ref, v_ref, g_ref, kx_ref, qx_ref, o_ref, qa_ref, ka_ref, vb_ref):
    seq = q_ref.shape[0]
    nb = seq // BLOCK
    q16 = q_ref[...].astype(BF16)
    k = k_ref[...]

    kmean = jnp.mean(k.reshape(nb, BLOCK, HEAD_DIM), axis=1)
    gate_t = _dot_nt(kmean.astype(BF16), q16)
    blk = lax.broadcasted_iota(jnp.int32, (nb, seq), 0)
    q_blk = lax.broadcasted_iota(jnp.int32, (nb, seq), 1) // BLOCK
    rank = jnp.zeros((nb, seq), jnp.int32)
    for other in range(nb):
        g_o = gate_t[other:other + 1, :]
        beats = (g_o > gate_t) | ((g_o == gate_t) & (other < blk))
        rank += jnp.where((q_blk > other) & beats, 1, 0)
    dropped = ((blk < q_blk) & (rank >= TOP_K)).astype(F32)
    lane = lax.broadcasted_iota(jnp.int32, (nb, LANES), 1)
    spread = jnp.where(lane == lax.broadcasted_iota(jnp.int32, (nb, LANES), 0), -MASK_BIG, 0.0)
    sel_bias = lax.dot_general(dropped, spread, (((0,), (0,)), ((), ())),
                               preferred_element_type=F32)

    qa_ref[:, :HEAD_DIM] = q16
    qa_ref[:, HEAD_DIM:] = (sel_bias + qx_ref[0, 0:1, :]).astype(BF16)
    ka_ref[:, :HEAD_DIM] = k.astype(BF16)
    ka_ref[:, HEAD_DIM:] = kx_ref[...]
    vb_ref[:, :HEAD_DIM] = v_ref[...].astype(BF16)
    ones_col = lax.broadcasted_iota(jnp.int32, (seq, HEAD_DIM), 1) == 0
    vb_ref[:, HEAD_DIM:] = jnp.where(ones_col, 1.0, 0.0).astype(BF16)

    r_i = lax.broadcasted_iota(jnp.int32, (BLOCK, BLOCK), 0)
    c_i = lax.broadcasted_iota(jnp.int32, (BLOCK, BLOCK), 1)
    causal = jnp.where(c_i <= r_i, 0.0, -jnp.inf)
    for i in range(nb):
        rows = slice(i * BLOCK, (i + 1) * BLOCK)
        n_keys = (i + 1) * BLOCK
        s = _dot_nt(qa_ref[rows, :], ka_ref[0:n_keys, :])
        s_own = s[:, i * BLOCK:] + causal
        s = s_own if i == 0 else jnp.concatenate([s[:, :i * BLOCK], s_own], axis=1)
        m = jnp.max(s, axis=-1, keepdims=True)
        p = jnp.exp2((s - m) * (SCALE * LOG2_E))
        ol = _dot(p.astype(BF16), vb_ref[0:n_keys, :])
        o = ol[:, :HEAD_DIM] / ol[:, HEAD_DIM:HEAD_DIM + 1]
        gate = g_ref[rows, :]
        o_ref[rows, :] = (o * (gate * jax.nn.sigmoid(gate))).astype(o_ref.dtype)


def _alibi_slopes():
    return 2.0 ** (-8.0 * jnp.arange(1, N_HEADS + 1, dtype=F32) / N_HEADS)


def _moba_prefill(q, k, v, g, n_batch, seq):
    nb = seq // BLOCK
    assert nb <= SUBLANES and seq % BLOCK == 0
    pos = np.arange(seq)
    kx = np.zeros((seq, LANES), np.float32)
    kx[pos, pos // BLOCK] = 1.0
    kx[:, SUBLANES:SUBLANES + ALIBI_PARTS] = (pos % BLOCK)[:, None]
    kx[:, SUBLANES + ALIBI_PARTS:SUBLANES + 2 * ALIBI_PARTS] = ((pos // BLOCK) * BLOCK)[:, None]
    rest = _alibi_slopes() / SCALE
    pieces = []
    for _ in range(ALIBI_PARTS):
        piece = rest.astype(BF16).astype(F32)
        pieces.append(piece)
        rest = rest - piece
    pieces = jnp.stack(pieces, axis=1)
    qx = jnp.zeros((N_HEADS, LANES), F32)
    qx = qx.at[:, SUBLANES:SUBLANES + ALIBI_PARTS].set(pieces)
    qx = qx.at[:, SUBLANES + ALIBI_PARTS:SUBLANES + 2 * ALIBI_PARTS].set(pieces)
    qx = jnp.broadcast_to(qx[:, None, :], (N_HEADS, SUBLANES, LANES))
    head_spec = pl.BlockSpec((seq, HEAD_DIM), lambda b, h: (b, h))
    return pl.pallas_call(
        _moba_prefill_kernel,
        out_shape=jax.ShapeDtypeStruct(q.shape, BF16),
        grid=(n_batch, N_HEADS),
        in_specs=[head_spec, head_spec, head_spec, head_spec,
                  pl.BlockSpec((seq, LANES), lambda b, h: (0, 0)),
                  pl.BlockSpec((1, SUBLANES, LANES), lambda b, h: (h, 0, 0))],
        out_specs=head_spec,
        scratch_shapes=[pltpu.VMEM((seq, 2 * HEAD_DIM), BF16),
                        pltpu.VMEM((seq, 2 * HEAD_DIM), BF16),
                        pltpu.VMEM((seq, 2 * HEAD_DIM), BF16)],
        compiler_params=_params("arbitrary", "arbitrary"),
        name="moba_prefill",
    )(q, k, v, g, jnp.asarray(kx, BF16), qx)


def _block_diag_rows(x, n_q):
    rows = N_HEADS * n_q
    tiled = jnp.concatenate([x] * N_HEADS, axis=0)
    r_h = lax.broadcasted_iota(jnp.int32, (rows, x.shape[1]), 0) // n_q
    c_h = lax.broadcasted_iota(jnp.int32, (rows, x.shape[1]), 1) // HEAD_DIM
    return jnp.where(r_h == c_h, tiled, 0.0)


def _pad_rows(x, rows):
    return jnp.concatenate([x, jnp.zeros((rows - x.shape[0], x.shape[1]), x.dtype)], axis=0)


def _gather_heads(page_refs, dst_ref):
    for i in range(PAGES_PER_STEP):
        for hg in range(HEAD_GROUPS):
            flat = page_refs[i * HEAD_GROUPS + hg].reshape(PAGE_SIZE * SUBLANES, HEAD_DIM)
            for s in range(SUBLANES):
                h = hg * SUBLANES + s
                dst_ref[i * PAGE_SIZE:(i + 1) * PAGE_SIZE, h * HEAD_DIM:(h + 1) * HEAD_DIM] = (
                    flat[pl.ds(s, PAGE_SIZE, stride=SUBLANES), :].astype(BF16))


def _moba_decode_scores_kernel(pt_ref, q_ref, knew_ref, slope_ref, *rest, n_q, past_len):
    pages = rest[:PAGES_PER_STEP * HEAD_GROUPS]
    p_ref, pown_ref, l_ref, s_ref, ksum_ref, kc_ref, qbd_ref = rest[PAGES_PER_STEP * HEAD_GROUPS:]
    c = pl.program_id(1)
    n_chunks = pl.num_programs(1)
    chunk = PAGES_PER_STEP * PAGE_SIZE
    blocks_per_chunk = chunk // BLOCK
    n_blocks = past_len // BLOCK
    rows = N_HEADS * n_q

    @pl.when(c == 0)
    def _():
        qbd_ref[...] = _block_diag_rows(q_ref[0], n_q).astype(BF16)

    _gather_heads(pages, kc_ref)
    pages_per_block = BLOCK // PAGE_SIZE
    for bi in range(blocks_per_chunk):
        for hg in range(HEAD_GROUPS):
            tot = None
            for i in range(bi * pages_per_block, (bi + 1) * pages_per_block):
                part = jnp.sum(pages[i * HEAD_GROUPS + hg][...], axis=0)
                tot = part if tot is None else tot + part
            row = ((c * blocks_per_chunk + bi) * HEAD_GROUPS + hg) * SUBLANES
            ksum_ref[pl.ds(pl.multiple_of(row, SUBLANES), SUBLANES), :] = tot
    s_ref[c] = _dot_nt(qbd_ref[...], kc_ref[...])

    @pl.when(c == n_chunks - 1)
    def _():
        qbd = qbd_ref[...]
        ksum = jnp.concatenate([ksum_ref[pl.ds(h, n_blocks, stride=N_HEADS), :]
                                for h in range(N_HEADS)], axis=1)
        kmean = (ksum / BLOCK).astype(BF16)
        gate = _dot_nt(qbd, kmean)
        idx = lax.broadcasted_iota(jnp.int32, gate.shape, 1)
        picked = jnp.zeros(gate.shape, jnp.bool_)
        for _ in range(TOP_K):
            best = jnp.max(gate, axis=-1, keepdims=True)
            first = jnp.min(jnp.where(gate == best, idx, n_blocks), axis=-1, keepdims=True)
            hit = idx == first
            picked = picked | hit
            gate = jnp.where(hit, -jnp.inf, gate)
        picked16 = picked.astype(BF16)

        slope = slope_ref[:, 0:1]
        q_i = lax.rem(lax.broadcasted_iota(jnp.int32, (rows, 1), 0), n_q)
        t = (past_len + q_i).astype(F32)

        def mask_chunk(ci, m_run):
            blk_of_col = ci * blocks_per_chunk + lax.broadcasted_iota(
                jnp.int32, (n_blocks, chunk), 1) // BLOCK
            expand = (lax.broadcasted_iota(jnp.int32, (n_blocks, chunk), 0) == blk_of_col).astype(BF16)
            keep = _dot(picked16, expand)
            pos = (ci * chunk + lax.broadcasted_iota(jnp.int32, (1, chunk), 1)).astype(F32)
            s = s_ref[ci] * SCALE - slope * (t - pos)
            s = jnp.where(keep > 0.5, s, -jnp.inf)
            s_ref[ci] = s
            return jnp.maximum(m_run, jnp.max(s, axis=-1, keepdims=True))

        m_past = lax.fori_loop(0, n_chunks, mask_chunk, jnp.full((rows, 1), -jnp.inf, F32))

        k_own = _pad_rows(knew_ref[0], LANES).astype(BF16)
        j_own = lax.broadcasted_iota(jnp.int32, (rows, LANES), 1)
        s_own = _dot_nt(qbd, k_own) * SCALE - slope * (q_i - j_own).astype(F32)
        s_own = jnp.where(j_own <= q_i, s_own, -jnp.inf)
        m_all = jnp.maximum(m_past, jnp.max(s_own, axis=-1, keepdims=True))
        p_own = jnp.exp(s_own - m_all)
        pown_ref[0] = p_own

        def exp_chunk(ci, l_run):
            p = jnp.exp(s_ref[ci] - m_all)
            p_ref[0, ci] = p.astype(p_ref.dtype)
            return l_run + jnp.sum(p, axis=-1, keepdims=True)

        l_all = lax.fori_loop(0, n_chunks, exp_chunk, jnp.sum(p_own, axis=-1, keepdims=True))
        l_ref[0] = jnp.broadcast_to(l_all, (rows, LANES))


def _moba_decode_values_kernel(pt_ref, p_ref, pown_ref, l_ref, vnew_ref, g_ref, *rest, n_q):
    pages = rest[:PAGES_PER_STEP * HEAD_GROUPS]
    o_ref, acc_ref, vc_ref = rest[PAGES_PER_STEP * HEAD_GROUPS:]
    c = pl.program_id(1)
    rows = N_HEADS * n_q

    _gather_heads(pages, vc_ref)
    part = _dot(p_ref[0, 0], vc_ref[...])

    @pl.when(c == 0)
    def _():
        acc_ref[...] = part

    @pl.when(c > 0)
    def _():
        acc_ref[...] += part

    @pl.when(c == pl.num_programs(1) - 1)
    def _():
        v_own = _pad_rows(vnew_ref[0], LANES).astype(BF16)
        full = (acc_ref[...] + _dot(pown_ref[0].astype(BF16), v_own)) / l_ref[0][:, 0:1]
        r_h = lax.broadcasted_iota(jnp.int32, full.shape, 0) // n_q
        c_h = lax.broadcasted_iota(jnp.int32, full.shape, 1) // HEAD_DIM
        o = jnp.sum(jnp.where(r_h == c_h, full, 0.0).reshape(N_HEADS, n_q, full.shape[1]), axis=0)
        gate = g_ref[0]
        o_ref[0] = o * (gate * jax.nn.sigmoid(gate))


def _moba_decode(q, g, k_new, v_new, cache_k, cache_v, page_table, n_batch, n_q):
    d = q.shape[1]
    n_pages = page_table.shape[1]
    past_len = n_pages * PAGE_SIZE
    assert n_q == SUBLANES and past_len % BLOCK == 0 and n_pages % PAGES_PER_STEP == 0
    n_chunks = n_pages // PAGES_PER_STEP
    chunk = PAGES_PER_STEP * PAGE_SIZE
    n_blocks = past_len // BLOCK
    rows = N_HEADS * n_q
    pt = page_table.reshape(-1).astype(jnp.int32)
    ck = cache_k.reshape(cache_k.shape[0] * PAGE_SIZE, N_HEADS, HEAD_DIM)
    cv = cache_v.reshape(cache_v.shape[0] * PAGE_SIZE, N_HEADS, HEAD_DIM)
    q3, g3 = q.reshape(n_batch, n_q, d), g.reshape(n_batch, n_q, d)
    kn3, vn3 = k_new.reshape(n_batch, n_q, d), v_new.reshape(n_batch, n_q, d)
    slope_rows = jnp.broadcast_to(jnp.repeat(_alibi_slopes(), n_q)[:, None], (rows, LANES))

    def page_spec(i, hg):
        return pl.BlockSpec(
            (PAGE_SIZE, SUBLANES, HEAD_DIM),
            lambda b, c, pt_ref: (pt_ref[b * n_pages + c * PAGES_PER_STEP + i], hg, 0))

    page_specs = [page_spec(i, hg) for i in range(PAGES_PER_STEP) for hg in range(HEAD_GROUPS)]

    tok_spec = pl.BlockSpec((1, n_q, d), lambda b, c, pt_ref: (b, 0, 0))
    row_spec = pl.BlockSpec((1, rows, LANES), lambda b, c, pt_ref: (b, 0, 0))
    probs, p_own, denom = pl.pallas_call(
        functools.partial(_moba_decode_scores_kernel, n_q=n_q, past_len=past_len),
        out_shape=(jax.ShapeDtypeStruct((n_batch, n_chunks, rows, chunk), BF16),
                   jax.ShapeDtypeStruct((n_batch, rows, LANES), F32),
                   jax.ShapeDtypeStruct((n_batch, rows, LANES), F32)),
        grid_spec=pltpu.PrefetchScalarGridSpec(
            num_scalar_prefetch=1,
            grid=(n_batch, n_chunks),
            in_specs=[tok_spec, tok_spec,
                      pl.BlockSpec((rows, LANES), lambda b, c, pt_ref: (0, 0))] + page_specs,
            out_specs=(pl.BlockSpec((1, n_chunks, rows, chunk), lambda b, c, pt_ref: (b, 0, 0, 0)),
                       row_spec, row_spec),
            scratch_shapes=[pltpu.VMEM((n_chunks, rows, chunk), F32),
                            pltpu.VMEM((n_blocks * N_HEADS, HEAD_DIM), F32),
                            pltpu.VMEM((chunk, d), BF16),
                            pltpu.VMEM((rows, d), BF16)]),
        compiler_params=_params("arbitrary", "arbitrary"),
        name="moba_decode_scores",
    )(pt, q3, kn3, slope_rows, *([ck] * len(page_specs)))

    out = pl.pallas_call(
        functools.partial(_moba_decode_values_kernel, n_q=n_q),
        out_shape=jax.ShapeDtypeStruct((n_batch, n_q, d), F32),
        grid_spec=pltpu.PrefetchScalarGridSpec(
            num_scalar_prefetch=1,
            grid=(n_batch, n_chunks),
            in_specs=[pl.BlockSpec((1, 1, rows, chunk), lambda b, c, pt_ref: (b, c, 0, 0)),
                      row_spec, row_spec, tok_spec, tok_spec] + page_specs,
            out_specs=tok_spec,
            scratch_shapes=[pltpu.VMEM((rows, d), F32),
                            pltpu.VMEM((chunk, d), BF16)]),
        compiler_params=_params("arbitrary", "arbitrary"),
        name="moba_decode_values",
    )(pt, probs, p_own, denom, vn3, g3, *([cv] * len(page_specs)))
    return out.reshape(n_batch * n_q, d).astype(BF16)


def _trunk(x, conv_prev, weights, attention):
    w_in_a, conv_w, w_out_a, w_kv, w_in_b, w_out_b, ln_g, ln_b = weights
    n_seq, t_len, d = x.shape
    x32 = x.reshape(n_seq * t_len, d)
    x16 = x32.astype(BF16)
    conv_rows = []
    for l in range(N_A):
        y16, rows = _conv_proj(x16, w_in_a, l, conv_w[l], conv_prev[l], t_len)
        conv_rows.append(rows)
        x32, x16 = _out_proj_ln(y16, w_out_a, l, x32, ln_g[l], ln_b[l])
    k_new, v_new = _dual_proj(x16, w_kv[None], 0)
    for l in range(N_A, DEPTH):
        q, g = _dual_proj(x16, w_in_b, l - N_A)
        o16 = attention(q, g, k_new, v_new)
        x32, x16 = _out_proj_ln(o16, w_out_b, l - N_A, x32, ln_g[l], ln_b[l])
    heads = (n_seq, t_len, N_HEADS, HEAD_DIM)
    return (x32.reshape(n_seq, t_len, d), k_new.reshape(heads), v_new.reshape(heads),
            jnp.stack(conv_rows))


def kernel(x_prompt, x_sample, cache_k, cache_v, state_conv, page_table, w_in_a, conv_w, w_out_a,
           w_kv, w_in_b, w_out_b, ln_g, ln_b):
    weights = (w_in_a.astype(BF16), conv_w, w_out_a.astype(BF16), w_kv.astype(BF16),
               w_in_b.astype(BF16), w_out_b.astype(BF16), ln_g, ln_b)
    n_prompt, seq, d = x_prompt.shape
    n_dec, dec_seq, _ = x_sample.shape

    zero_conv = jnp.zeros((N_A, n_prompt, CONV_W - 1, d), x_prompt.dtype)
    y_prompt, k_prompt, v_prompt, conv_prompt = _trunk(
        x_prompt, zero_conv, weights,
        lambda q, g, k, v: _moba_prefill(q, k, v, g, n_prompt, seq))
    y_sample, k_sample, v_sample, conv_sample = _trunk(
        x_sample, state_conv, weights,
        lambda q, g, k, v: _moba_decode(q, g, k, v, cache_k, cache_v, page_table, n_dec, dec_seq))
    return (y_prompt, y_sample, k_prompt, v_prompt, conv_prompt, k_sample, v_sample, conv_sample)
```

```python
import functools
import math

import numpy as np
import jax
import jax.numpy as jnp
from jax import lax
from jax.experimental import pallas as pl
from jax.experimental.pallas import tpu as pltpu

N_HEADS = 32
HEAD_DIM = 128
CONV_W = 3
BLOCK = 256
TOP_K = 3
PAGE_SIZE = 128
DEPTH = 4
N_A = DEPTH // 2
ALPHA = (2.0 * DEPTH) ** 0.25
LN_EPS = 1e-5
SCALE = 1.0 / math.sqrt(HEAD_DIM)
LOG2_E = math.log2(math.e)

SUBLANES = 8
LANES = 128
VMEM_LIMIT_BYTES = 56 * 1024 * 1024
MASK_BIG = 2.0 ** 100
SCORE_PAGES_PER_STEP = 4
VALUE_PAGES_PER_STEP = 8
HEAD_GROUPS = N_HEADS // SUBLANES
ALIBI_PARTS = 4

F32 = jnp.float32
BF16 = jnp.bfloat16


def _params(*semantics):
    return pltpu.CompilerParams(dimension_semantics=semantics, vmem_limit_bytes=VMEM_LIMIT_BYTES)


def _dot(a, b):
    return jnp.dot(a, b, preferred_element_type=F32)


def _dot_nt(a, b):
    return lax.dot_general(a, b, (((1,), (1,)), ((), ())), preferred_element_type=F32)


def _gated_conv(b_g, u, u_m1, u_m2, z, cw_ref):
    conv = cw_ref[0:1, :] * u_m2 + cw_ref[1:2, :] * u_m1 + cw_ref[2:3, :] * u
    return (b_g * conv) * (z * jax.nn.sigmoid(z))


def _conv_proj_long_kernel(x_ref, wb_ref, wc_ref, wh_ref, wz_ref, cw_ref, prev_ref,
                           y_ref, tail_ref, carry_ref, *, tiles_per_seq):
    m = pl.program_id(1)

    @pl.when(lax.rem(m, tiles_per_seq) == 0)
    def _():
        carry_ref[...] = prev_ref[0]

    x = x_ref[...]
    u = _dot(x, wc_ref[...]) * _dot(x, wh_ref[...])
    tm = u.shape[0]
    row = lax.broadcasted_iota(jnp.int32, u.shape, 0)
    c_m2 = carry_ref[SUBLANES - 2:SUBLANES - 1, :]
    c_m1 = carry_ref[SUBLANES - 1:SUBLANES, :]
    u_m1 = jnp.where(row == 0, c_m1, pltpu.roll(u, 1, 0))
    u_m2 = jnp.where(row == 0, c_m2, jnp.where(row == 1, c_m1, pltpu.roll(u, 2, 0)))
    y = _gated_conv(_dot(x, wb_ref[...]), u, u_m1, u_m2, _dot(x, wz_ref[...]), cw_ref)
    y_ref[...] = y.astype(y_ref.dtype)
    tail = u[tm - SUBLANES:, :]
    carry_ref[...] = tail
    tail_ref[0] = tail


def _conv_proj_short_kernel(x_ref, wb_ref, wc_ref, wh_ref, wz_ref, cw_ref, pm1_ref, pm2_ref,
                            y_ref, u_ref, *, seq_len):
    x = x_ref[...]
    u = _dot(x, wc_ref[...]) * _dot(x, wh_ref[...])
    pos = lax.rem(lax.broadcasted_iota(jnp.int32, u.shape, 0), seq_len)
    u_m1 = jnp.where(pos == 0, pm1_ref[...], pltpu.roll(u, 1, 0))
    u_m2 = jnp.where(pos < 2, pm2_ref[...], pltpu.roll(u, 2, 0))
    y = _gated_conv(_dot(x, wb_ref[...]), u, u_m1, u_m2, _dot(x, wz_ref[...]), cw_ref)
    y_ref[...] = y.astype(y_ref.dtype)
    u_ref[...] = u


def _conv_proj(x16, w_in16, layer, conv_w, conv_prev, seq_len):
    m_rows, d = x16.shape
    n_seq = m_rows // seq_len
    tn = 256
    nj = d // tn
    w_specs = [pl.BlockSpec((None, d, tn),
                            functools.partial(lambda j, m, g: (layer, 0, g * nj + j), g=g))
               for g in range(4)]
    cw_spec = pl.BlockSpec((CONV_W, tn), lambda j, m: (0, j))
    if seq_len >= 512:
        tm = 1024 if seq_len % 1024 == 0 else 512
        tiles_per_seq = seq_len // tm
        prev8 = jnp.pad(conv_prev.astype(F32), ((0, 0), (SUBLANES - (CONV_W - 1), 0), (0, 0)))
        y, tail = pl.pallas_call(
            functools.partial(_conv_proj_long_kernel, tiles_per_seq=tiles_per_seq),
            out_shape=(jax.ShapeDtypeStruct((m_rows, d), BF16),
                       jax.ShapeDtypeStruct((n_seq, SUBLANES, d), F32)),
            grid=(nj, m_rows // tm),
            in_specs=[pl.BlockSpec((tm, d), lambda j, m: (m, 0))] + w_specs + [
                cw_spec,
                pl.BlockSpec((1, SUBLANES, tn), lambda j, m: (m // tiles_per_seq, 0, j))],
            out_specs=(pl.BlockSpec((tm, tn), lambda j, m: (m, j)),
                       pl.BlockSpec((1, SUBLANES, tn), lambda j, m: (m // tiles_per_seq, 0, j))),
            scratch_shapes=[pltpu.VMEM((SUBLANES, tn), F32)],
            compiler_params=_params("arbitrary", "arbitrary"),
            name="conv_proj_long",
        )(x16, w_in16, w_in16, w_in16, w_in16, conv_w, prev8)
        return y, tail[:, SUBLANES - (CONV_W - 1):, :]

    prev = conv_prev.astype(F32)
    zeros = jnp.zeros((n_seq, seq_len - 1, d), F32)
    pm1 = jnp.concatenate([prev[:, 1:2], zeros], axis=1).reshape(m_rows, d)
    pm2 = jnp.concatenate([prev, zeros[:, 1:]], axis=1).reshape(m_rows, d)
    y, u = pl.pallas_call(
        functools.partial(_conv_proj_short_kernel, seq_len=seq_len),
        out_shape=(jax.ShapeDtypeStruct((m_rows, d), BF16),
                   jax.ShapeDtypeStruct((m_rows, d), F32)),
        grid=(nj, 1),
        in_specs=[pl.BlockSpec((m_rows, d), lambda j, m: (0, 0))] + w_specs + [
            cw_spec,
            pl.BlockSpec((m_rows, tn), lambda j, m: (0, j)),
            pl.BlockSpec((m_rows, tn), lambda j, m: (0, j))],
        out_specs=(pl.BlockSpec((m_rows, tn), lambda j, m: (0, j)),
                   pl.BlockSpec((m_rows, tn), lambda j, m: (0, j))),
        compiler_params=_params("arbitrary", "arbitrary"),
        name="conv_proj_short",
    )(x16, w_in16, w_in16, w_in16, w_in16, conv_w, pm1, pm2)
    return y, u.reshape(n_seq, seq_len, d)[:, seq_len - (CONV_W - 1):, :]


def _fold_lanes(v):
    out = v[:, :LANES]
    for t in range(1, v.shape[1] // LANES):
        out = out + v[:, t * LANES:(t + 1) * LANES]
    return out


def _out_proj_ln_kernel(y_ref, w_ref, xres_ref, g_ref, b_ref, xo_ref, xo16_ref,
                        r_ref, shift_ref, sum_ref, sq_ref, mean_ref, rstd_ref, *, nm, nj, d):
    mm = pl.program_id(0)
    j = pl.program_id(1)

    @pl.when(mm > 0)
    def _():
        o = (r_ref[j] - mean_ref[:, 0:1]) * rstd_ref[:, 0:1] * g_ref[...] + b_ref[...]
        xo_ref[...] = o
        xo16_ref[...] = o.astype(xo16_ref.dtype)

    @pl.when(mm < nm)
    def _():
        r = ALPHA * xres_ref[...] + _dot(y_ref[...], w_ref[...])
        r_ref[j] = r

        @pl.when(j == 0)
        def _():
            shift_ref[...] = jnp.broadcast_to(jnp.mean(r, axis=-1, keepdims=True), shift_ref.shape)
            sum_ref[...] = jnp.zeros_like(sum_ref)
            sq_ref[...] = jnp.zeros_like(sq_ref)

        dlt = r - shift_ref[:, 0:1]
        sum_ref[...] += _fold_lanes(dlt)
        sq_ref[...] += _fold_lanes(dlt * dlt)

        @pl.when(j == nj - 1)
        def _():
            e1 = jnp.sum(sum_ref[...], axis=-1, keepdims=True) / d
            e2 = jnp.sum(sq_ref[...], axis=-1, keepdims=True) / d
            var = jnp.maximum(e2 - e1 * e1, 0.0)
            mean_ref[...] = jnp.broadcast_to(shift_ref[:, 0:1] + e1, mean_ref.shape)
            rstd_ref[...] = jnp.broadcast_to(lax.rsqrt(var + LN_EPS), rstd_ref.shape)


def _out_proj_ln(y16, w16, layer, x_res, gain, bias):
    m_rows, k = y16.shape
    d = w16.shape[2]
    tm = min(1024, m_rows)
    tn = 512
    assert m_rows % tm == 0 and d % tn == 0
    nj = d // tn
    nm = m_rows // tm
    row_in = lambda mm: jnp.minimum(mm, nm - 1)
    col_in = lambda mm, j: jnp.where(mm < nm, j, nj - 1)
    out_idx = lambda mm, j: (jnp.maximum(mm - 1, 0), jnp.where(mm == 0, 0, j))
    stat = pltpu.VMEM((tm, LANES), F32)
    return pl.pallas_call(
        functools.partial(_out_proj_ln_kernel, nm=nm, nj=nj, d=d),
        out_shape=(jax.ShapeDtypeStruct((m_rows, d), F32), jax.ShapeDtypeStruct((m_rows, d), BF16)),
        grid=(nm + 1, nj),
        in_specs=[pl.BlockSpec((tm, k), lambda mm, j: (row_in(mm), 0)),
                  pl.BlockSpec((None, k, tn), lambda mm, j: (layer, 0, col_in(mm, j))),
                  pl.BlockSpec((tm, tn), lambda mm, j: (row_in(mm), col_in(mm, j))),
                  pl.BlockSpec((1, tn), lambda mm, j: (0, j)),
                  pl.BlockSpec((1, tn), lambda mm, j: (0, j))],
        out_specs=(pl.BlockSpec((tm, tn), out_idx), pl.BlockSpec((tm, tn), out_idx)),
        scratch_shapes=[pltpu.VMEM((nj, tm, tn), F32), stat, stat, stat, stat, stat],
        compiler_params=_params("arbitrary", "arbitrary"),
        name="out_proj_ln",
    )(y16, w16, x_res, gain.reshape(1, d), bias.reshape(1, d))


def _dual_proj_kernel(x_ref, wa_ref, wb_ref, oa_ref, ob_ref):
    x = x_ref[...]
    oa_ref[...] = _dot(x, wa_ref[...])
    ob_ref[...] = _dot(x, wb_ref[...])


def _dual_proj(x16, w16, layer):
    m_rows, k = x16.shape
    n_half = w16.shape[2] // 2
    tm = next(t for t in (1024, 512, m_rows) if m_rows % t == 0)
    tn = 512
    nj = n_half // tn
    out = jax.ShapeDtypeStruct((m_rows, n_half), F32)
    return pl.pallas_call(
        _dual_proj_kernel,
        out_shape=(out, out),
        grid=(m_rows // tm, nj),
        in_specs=[pl.BlockSpec((tm, k), lambda m, j: (m, 0)),
                  pl.BlockSpec((None, k, tn), lambda m, j: (layer, 0, j)),
                  pl.BlockSpec((None, k, tn), lambda m, j: (layer, 0, nj + j))],
        out_specs=(pl.BlockSpec((tm, tn), lambda m, j: (m, j)),
                   pl.BlockSpec((tm, tn), lambda m, j: (m, j))),
        compiler_params=_params("arbitrary", "arbitrary"),
        name="dual_proj",
    )(x16, w16, w16)


def _moba_prefill_kernel(q_ref, k_ref, v_ref, g_ref, kx_ref, qx_ref, o_ref,
                         qa_ref, ka_ref, vb_ref, s_ref):
    seq = q_ref.shape[0]
    nb = seq // BLOCK
    q16 = q_ref[...].astype(BF16)
    k = k_ref[...]

    kmean = jnp.mean(k.reshape(nb, BLOCK, HEAD_DIM), axis=1)
    gate_t = _dot_nt(kmean.astype(BF16), q16)
    blk = lax.broadcasted_iota(jnp.int32, (nb, seq), 0)
    q_blk = lax.broadcasted_iota(jnp.int32, (nb, seq), 1) // BLOCK
    rank = jnp.zeros((nb, seq), jnp.int32)
    for other in range(nb):
        g_o = gate_t[other:other + 1, :]
        beats = (g_o > gate_t) | ((g_o == gate_t) & (other < blk))
        rank += jnp.where((q_blk > other) & beats, 1, 0)
    dropped = ((blk < q_blk) & (rank >= TOP_K)).astype(F32)
    lane = lax.broadcasted_iota(jnp.int32, (nb, LANES), 1)
    spread = jnp.where(lane == lax.broadcasted_iota(jnp.int32, (nb, LANES), 0), -MASK_BIG, 0.0)
    sel_bias = lax.dot_general(dropped, spread, (((0,), (0,)), ((), ())),
                               preferred_element_type=F32)

    qa_ref[:, :HEAD_DIM] = q16
    qa_ref[:, HEAD_DIM:] = (sel_bias + qx_ref[0, 0:1, :]).astype(BF16)
    ka_ref[:, :HEAD_DIM] = k.astype(BF16)
    ka_ref[:, HEAD_DIM:] = kx_ref[...]
    vb_ref[:, :HEAD_DIM] = v_ref[...].astype(BF16)
    ones_col = lax.broadcasted_iota(jnp.int32, (seq, HEAD_DIM), 1) == 0
    vb_ref[:, HEAD_DIM:] = jnp.where(ones_col, 1.0, 0.0).astype(BF16)

    r_i = lax.broadcasted_iota(jnp.int32, (BLOCK, BLOCK), 0)
    c_i = lax.broadcasted_iota(jnp.int32, (BLOCK, BLOCK), 1)
    causal = jnp.where(c_i <= r_i, 0.0, -jnp.inf)
    for i in range(nb):
        s_ref[i, i * BLOCK:, :] = _dot_nt(qa_ref[i * BLOCK:, :], ka_ref[i * BLOCK:(i + 1) * BLOCK, :])
        rows = slice(i * BLOCK, (i + 1) * BLOCK)
        n_keys = (i + 1) * BLOCK
        parts = [s_ref[n, rows, :] for n in range(i)] + [s_ref[i, rows, :] + causal]
        s = parts[0] if i == 0 else jnp.concatenate(parts, axis=1)
        m = jnp.max(s, axis=-1, keepdims=True)
        p = jnp.exp2((s - m) * (SCALE * LOG2_E))
        ol = _dot(p.astype(BF16), vb_ref[0:n_keys, :])
        o = ol[:, :HEAD_DIM] / ol[:, HEAD_DIM:HEAD_DIM + 1]
        gate = g_ref[rows, :]
        o_ref[rows, :] = (o * (gate * jax.nn.sigmoid(gate))).astype(o_ref.dtype)


def _alibi_slopes():
    return 2.0 ** (-8.0 * jnp.arange(1, N_HEADS + 1, dtype=F32) / N_HEADS)


def _moba_prefill(q, k, v, g, n_batch, seq):
    nb = seq // BLOCK
    assert nb <= SUBLANES and seq % BLOCK == 0
    pos = np.arange(seq)
    kx = np.zeros((seq, LANES), np.float32)
    kx[pos, pos // BLOCK] = 1.0
    kx[:, SUBLANES:SUBLANES + ALIBI_PARTS] = (pos % BLOCK)[:, None]
    kx[:, SUBLANES + ALIBI_PARTS:SUBLANES + 2 * ALIBI_PARTS] = ((pos // BLOCK) * BLOCK)[:, None]
    rest = _alibi_slopes() / SCALE
    pieces = []
    for _ in range(ALIBI_PARTS):
        piece = rest.astype(BF16).astype(F32)
        pieces.append(piece)
        rest = rest - piece
    pieces = jnp.stack(pieces, axis=1)
    qx = jnp.zeros((N_HEADS, LANES), F32)
    qx = qx.at[:, SUBLANES:SUBLANES + ALIBI_PARTS].set(pieces)
    qx = qx.at[:, SUBLANES + ALIBI_PARTS:SUBLANES + 2 * ALIBI_PARTS].set(pieces)
    qx = jnp.broadcast_to(qx[:, None, :], (N_HEADS, SUBLANES, LANES))
    head_spec = pl.BlockSpec((seq, HEAD_DIM), lambda b, h: (b, h))
    return pl.pallas_call(
        _moba_prefill_kernel,
        out_shape=jax.ShapeDtypeStruct(q.shape, BF16),
        grid=(n_batch, N_HEADS),
        in_specs=[head_spec, head_spec, head_spec, head_spec,
                  pl.BlockSpec((seq, LANES), lambda b, h: (0, 0)),
                  pl.BlockSpec((1, SUBLANES, LANES), lambda b, h: (h, 0, 0))],
        out_specs=head_spec,
        scratch_shapes=[pltpu.VMEM((seq, 2 * HEAD_DIM), BF16),
                        pltpu.VMEM((seq, 2 * HEAD_DIM), BF16),
                        pltpu.VMEM((seq, 2 * HEAD_DIM), BF16),
                        pltpu.VMEM((nb, seq, BLOCK), F32)],
        compiler_params=_params("arbitrary", "arbitrary"),
        name="moba_prefill",
    )(q, k, v, g, jnp.asarray(kx, BF16), qx)


def _block_diag_rows(x, n_q):
    rows = N_HEADS * n_q
    tiled = jnp.concatenate([x] * N_HEADS, axis=0)
    r_h = lax.broadcasted_iota(jnp.int32, (rows, x.shape[1]), 0) // n_q
    c_h = lax.broadcasted_iota(jnp.int32, (rows, x.shape[1]), 1) // HEAD_DIM
    return jnp.where(r_h == c_h, tiled, 0.0)


def _pad_rows(x, rows):
    return jnp.concatenate([x, jnp.zeros((rows - x.shape[0], x.shape[1]), x.dtype)], axis=0)


def _gather_heads(page_refs, dst_ref):
    for i in range(len(page_refs) // HEAD_GROUPS):
        for hg in range(HEAD_GROUPS):
            flat = page_refs[i * HEAD_GROUPS + hg].reshape(PAGE_SIZE * SUBLANES, HEAD_DIM)
            for s in range(SUBLANES):
                h = hg * SUBLANES + s
                dst_ref[i * PAGE_SIZE:(i + 1) * PAGE_SIZE, h * HEAD_DIM:(h + 1) * HEAD_DIM] = (
                    flat[pl.ds(s, PAGE_SIZE, stride=SUBLANES), :].astype(BF16))


def _moba_decode_scores_kernel(pt_ref, q_ref, knew_ref, slope_ref, *rest, n_q, past_len):
    n_in = SCORE_PAGES_PER_STEP * HEAD_GROUPS
    pages = rest[:n_in]
    p_ref, pown_ref, l_ref, s_ref, ksum_ref, kc_ref, qbd_ref, pick_ref = rest[n_in:]
    c = pl.program_id(1)
    n_chunks = pl.num_programs(1)
    chunk = SCORE_PAGES_PER_STEP * PAGE_SIZE
    blocks_per_chunk = chunk // BLOCK
    n_blocks = past_len // BLOCK
    rows = N_HEADS * n_q

    @pl.when(c == 0)
    def _():
        qbd_ref[...] = _block_diag_rows(q_ref[0], n_q).T.astype(BF16)

    _gather_heads(pages, kc_ref)
    pages_per_block = BLOCK // PAGE_SIZE
    for bi in range(blocks_per_chunk):
        for hg in range(HEAD_GROUPS):
            tot = None
            for i in range(bi * pages_per_block, (bi + 1) * pages_per_block):
                part = jnp.sum(pages[i * HEAD_GROUPS + hg][...], axis=0)
                tot = part if tot is None else tot + part
            row = ((c * blocks_per_chunk + bi) * HEAD_GROUPS + hg) * SUBLANES
            ksum_ref[pl.ds(pl.multiple_of(row, SUBLANES), SUBLANES), :] = tot
    s_ref[c] = _dot(kc_ref[...], qbd_ref[...])

    @pl.when(c == n_chunks - 1)
    def _():
        qbd = qbd_ref[...]
        ksum = jnp.concatenate([ksum_ref[pl.ds(h, n_blocks, stride=N_HEADS), :]
                                for h in range(N_HEADS)], axis=1)
        kmean = (ksum / BLOCK).astype(BF16)
        gate = _dot(kmean, qbd)
        idx = lax.broadcasted_iota(jnp.int32, gate.shape, 0)
        picked = jnp.zeros(gate.shape, jnp.bool_)
        for _ in range(TOP_K):
            best = jnp.max(gate, axis=0, keepdims=True)
            first = jnp.min(jnp.where(gate == best, idx, n_blocks), axis=0, keepdims=True)
            hit = idx == first
            picked = picked | hit
            gate = jnp.where(hit, -jnp.inf, gate)
        pick_ref[...] = picked.astype(F32)

        slope = slope_ref[0:1, :]
        q_i = lax.rem(lax.broadcasted_iota(jnp.int32, (1, rows), 1), n_q)
        t = (past_len + q_i).astype(F32)

        def mask_chunk(ci, m_run):
            pos = (ci * chunk + lax.broadcasted_iota(jnp.int32, (chunk, 1), 0)).astype(F32)
            s = s_ref[ci] * SCALE - slope * (t - pos)
            parts = []
            for bi in range(blocks_per_chunk):
                keep = pick_ref[pl.ds(ci * blocks_per_chunk + bi, 1), :]
                parts.append(jnp.where(keep > 0.5, s[bi * BLOCK:(bi + 1) * BLOCK], -jnp.inf))
            s = jnp.concatenate(parts, axis=0)
            s_ref[ci] = s
            return jnp.maximum(m_run, jnp.max(s, axis=0, keepdims=True))

        m_past = lax.fori_loop(0, n_chunks, mask_chunk, jnp.full((1, rows), -jnp.inf, F32))

        k_own = _pad_rows(knew_ref[0], LANES).astype(BF16)
        j_own = lax.broadcasted_iota(jnp.int32, (LANES, rows), 0)
        s_own = _dot(k_own, qbd) * SCALE - slope * (q_i - j_own).astype(F32)
        s_own = jnp.where(j_own <= q_i, s_own, -jnp.inf)
        m_all = jnp.maximum(m_past, jnp.max(s_own, axis=0, keepdims=True))
        p_own = jnp.exp(s_own - m_all)
        pown_ref[0] = p_own.T

        def exp_chunk(ci, l_run):
            p = jnp.exp(s_ref[ci] - m_all)
            p_ref[0, ci] = p.T.astype(p_ref.dtype)
            return l_run + jnp.sum(p, axis=0, keepdims=True)

        l_all = lax.fori_loop(0, n_chunks, exp_chunk, jnp.sum(p_own, axis=0, keepdims=True))
        l_ref[0] = jnp.broadcast_to(l_all, (LANES, rows)).T


def _moba_decode_values_kernel(pt_ref, p_ref, pown_ref, l_ref, vnew_ref, g_ref, *rest, n_q):
    n_in = VALUE_PAGES_PER_STEP * HEAD_GROUPS
    pages = rest[:n_in]
    o_ref, acc_ref, vc_ref = rest[n_in:]
    c = pl.program_id(1)
    score_chunk = SCORE_PAGES_PER_STEP * PAGE_SIZE

    _gather_heads(pages, vc_ref)
    part = None
    for sub in range(VALUE_PAGES_PER_STEP // SCORE_PAGES_PER_STEP):
        term = _dot(p_ref[0, sub], vc_ref[sub * score_chunk:(sub + 1) * score_chunk, :])
        part = term if part is None else part + term

    @pl.when(c == 0)
    def _():
        acc_ref[...] = part

    @pl.when(c > 0)
    def _():
        acc_ref[...] += part

    @pl.when(c == pl.num_programs(1) - 1)
    def _():
        v_own = _pad_rows(vnew_ref[0], LANES).astype(BF16)
        full = (acc_ref[...] + _dot(pown_ref[0].astype(BF16), v_own)) / l_ref[0][:, 0:1]
        r_h = lax.broadcasted_iota(jnp.int32, full.shape, 0) // n_q
        c_h = lax.broadcasted_iota(jnp.int32, full.shape, 1) // HEAD_DIM
        o = jnp.sum(jnp.where(r_h == c_h, full, 0.0).reshape(N_HEADS, n_q, full.shape[1]), axis=0)
        gate = g_ref[0]
        o_ref[0] = o * (gate * jax.nn.sigmoid(gate))


def _moba_decode(q, g, k_new, v_new, cache_k, cache_v, page_table, n_batch, n_q):
    d = q.shape[1]
    n_pages = page_table.shape[1]
    past_len = n_pages * PAGE_SIZE
    assert n_q == SUBLANES and past_len % BLOCK == 0 and n_pages % VALUE_PAGES_PER_STEP == 0
    assert VALUE_PAGES_PER_STEP % SCORE_PAGES_PER_STEP == 0
    n_chunks = n_pages // SCORE_PAGES_PER_STEP
    chunk = SCORE_PAGES_PER_STEP * PAGE_SIZE
    chunks_per_value_step = VALUE_PAGES_PER_STEP // SCORE_PAGES_PER_STEP
    n_blocks = past_len // BLOCK
    rows = N_HEADS * n_q
    pt = page_table.reshape(-1).astype(jnp.int32)
    ck = cache_k.reshape(cache_k.shape[0] * PAGE_SIZE, N_HEADS, HEAD_DIM)
    cv = cache_v.reshape(cache_v.shape[0] * PAGE_SIZE, N_HEADS, HEAD_DIM)
    q3, g3 = q.reshape(n_batch, n_q, d), g.reshape(n_batch, n_q, d)
    kn3, vn3 = k_new.reshape(n_batch, n_q, d), v_new.reshape(n_batch, n_q, d)
    slope_cols = jnp.broadcast_to(jnp.repeat(_alibi_slopes(), n_q)[None, :], (SUBLANES, rows))

    def page_specs(pages_per_step):
        def one(i, hg):
            return pl.BlockSpec(
                (PAGE_SIZE, SUBLANES, HEAD_DIM),
                lambda b, c, pt_ref: (pt_ref[b * n_pages + c * pages_per_step + i], hg, 0))
        return [one(i, hg) for i in range(pages_per_step) for hg in range(HEAD_GROUPS)]

    tok_spec = pl.BlockSpec((1, n_q, d), lambda b, c, pt_ref: (b, 0, 0))
    row_spec = pl.BlockSpec((1, rows, LANES), lambda b, c, pt_ref: (b, 0, 0))
    k_specs = page_specs(SCORE_PAGES_PER_STEP)
    probs, p_own, denom = pl.pallas_call(
        functools.partial(_moba_decode_scores_kernel, n_q=n_q, past_len=past_len),
        out_shape=(jax.ShapeDtypeStruct((n_batch, n_chunks, rows, chunk), BF16),
                   jax.ShapeDtypeStruct((n_batch, rows, LANES), F32),
                   jax.ShapeDtypeStruct((n_batch, rows, LANES), F32)),
        grid_spec=pltpu.PrefetchScalarGridSpec(
            num_scalar_prefetch=1,
            grid=(n_batch, n_chunks),
            in_specs=[tok_spec, tok_spec,
                      pl.BlockSpec((SUBLANES, rows), lambda b, c, pt_ref: (0, 0))] + k_specs,
            out_specs=(pl.BlockSpec((1, n_chunks, rows, chunk), lambda b, c, pt_ref: (b, 0, 0, 0)),
                       row_spec, row_spec),
            scratch_shapes=[pltpu.VMEM((n_chunks, chunk, rows), F32),
                            pltpu.VMEM((n_blocks * N_HEADS, HEAD_DIM), F32),
                            pltpu.VMEM((chunk, d), BF16),
                            pltpu.VMEM((d, rows), BF16),
                            pltpu.VMEM((n_blocks, rows), F32)]),
        compiler_params=_params("arbitrary", "arbitrary"),
        name="moba_decode_scores",
    )(pt, q3, kn3, slope_cols, *([ck] * len(k_specs)))

    v_specs = page_specs(VALUE_PAGES_PER_STEP)
    out = pl.pallas_call(
        functools.partial(_moba_decode_values_kernel, n_q=n_q),
        out_shape=jax.ShapeDtypeStruct((n_batch, n_q, d), F32),
        grid_spec=pltpu.PrefetchScalarGridSpec(
            num_scalar_prefetch=1,
            grid=(n_batch, n_pages // VALUE_PAGES_PER_STEP),
            in_specs=[pl.BlockSpec((1, chunks_per_value_step, rows, chunk),
                                   lambda b, c, pt_ref: (b, c, 0, 0)),
                      row_spec, row_spec, tok_spec, tok_spec] + v_specs,
            out_specs=tok_spec,
            scratch_shapes=[pltpu.VMEM((rows, d), F32),
                            pltpu.VMEM((VALUE_PAGES_PER_STEP * PAGE_SIZE, d), BF16)]),
        compiler_params=_params("arbitrary", "arbitrary"),
        name="moba_decode_values",
    )(pt, probs, p_own, denom, vn3, g3, *([cv] * len(v_specs)))
    return out.reshape(n_batch * n_q, d).astype(BF16)


def _trunk(x, conv_prev, weights, attention):
    w_in_a, conv_w, w_out_a, w_kv, w_in_b, w_out_b, ln_g, ln_b = weights
    n_seq, t_len, d = x.shape
    x32 = x.reshape(n_seq * t_len, d)
    x16 = x32.astype(BF16)
    conv_rows = []
    for l in range(N_A):
        y16, rows = _conv_proj(x16, w_in_a, l, conv_w[l], conv_prev[l], t_len)
        conv_rows.append(rows)
        x32, x16 = _out_proj_ln(y16, w_out_a, l, x32, ln_g[l], ln_b[l])
    k_new, v_new = _dual_proj(x16, w_kv[None], 0)
    for l in range(N_A, DEPTH):
        q, g = _dual_proj(x16, w_in_b, l - N_A)
        o16 = attention(q, g, k_new, v_new)
        x32, x16 = _out_proj_ln(o16, w_out_b, l - N_A, x32, ln_g[l], ln_b[l])
    heads = (n_seq, t_len, N_HEADS, HEAD_DIM)
    return (x32.reshape(n_seq, t_len, d), k_new.reshape(heads), v_new.reshape(heads),
            jnp.stack(conv_rows))


def kernel(x_prompt, x_sample, cache_k, cache_v, state_conv, page_table, w_in_a, conv_w, w_out_a,
           w_kv, w_in_b, w_out_b, ln_g, ln_b):
    weights = (w_in_a.astype(BF16), conv_w, w_out_a.astype(BF16), w_kv.astype(BF16),
               w_in_b.astype(BF16), w_out_b.astype(BF16), ln_g, ln_b)
    n_prompt, seq, d = x_prompt.shape
    n_dec, dec_seq, _ = x_sample.shape

    zero_conv = jnp.zeros((N_A, n_prompt, CONV_W - 1, d), x_prompt.dtype)
    y_prompt, k_prompt, v_prompt, conv_prompt = _trunk(
        x_prompt, zero_conv, weights,
        lambda q, g, k, v: _moba_prefill(q, k, v, g, n_prompt, seq))
    y_sample, k_sample, v_sample, conv_sample = _trunk(
        x_sample, state_conv, weights,
        lambda q, g, k, v: _moba_decode(q, g, k, v, cache_k, cache_v, page_table, n_dec, dec_seq))
    return (y_prompt, y_sample, k_prompt, v_prompt, conv_prompt, k_sample, v_sample, conv_sample)
```

```python
import functools
import math

import numpy as np
import jax
import jax.numpy as jnp
from jax import lax
from jax.experimental import pallas as pl
from jax.experimental.pallas import tpu as pltpu

N_HEADS = 32
HEAD_DIM = 128
CONV_W = 3
BLOCK = 256
TOP_K = 3
PAGE_SIZE = 128
DEPTH = 4
N_A = DEPTH // 2
ALPHA = (2.0 * DEPTH) ** 0.25
LN_EPS = 1e-5
SCALE = 1.0 / math.sqrt(HEAD_DIM)
LOG2_E = math.log2(math.e)

SUBLANES = 8
LANES = 128
VMEM_LIMIT_BYTES = 56 * 1024 * 1024
MASK_BIG = 2.0 ** 100
SCORE_PAGES_PER_STEP = 4
VALUE_PAGES_PER_STEP = 8
HEAD_GROUPS = N_HEADS // SUBLANES
ALIBI_PARTS = 4

F32 = jnp.float32
BF16 = jnp.bfloat16


def _params(*semantics):
    return pltpu.CompilerParams(dimension_semantics=semantics, vmem_limit_bytes=VMEM_LIMIT_BYTES)


def _dot(a, b):
    return jnp.dot(a, b, preferred_element_type=F32)


def _dot_nt(a, b):
    return lax.dot_general(a, b, (((1,), (1,)), ((), ())), preferred_element_type=F32)


def _cast_jobs(jobs, n_steps, step_of):
    in_specs, out_specs, out_shapes, operands = [], [], [], []
    for w, layer in jobs:
        _, r, c = w.shape
        assert r % n_steps == 0 and (r // n_steps) % 16 == 0
        slab = r // n_steps
        in_specs.append(pl.BlockSpec(
            (None, slab, c), functools.partial(lambda *ids, layer: (layer, step_of(*ids), 0), layer=layer)))
        out_specs.append(pl.BlockSpec((None, slab, c), lambda *ids: (0, step_of(*ids), 0)))
        out_shapes.append(jax.ShapeDtypeStruct((1, r, c), BF16))
        operands.append(w)
    return in_specs, out_specs, out_shapes, operands


def _run_casts(src_refs, dst_refs):
    for src, dst in zip(src_refs, dst_refs):
        dst[...] = src[...].astype(BF16)


def _gated_conv(b_g, u, u_m1, u_m2, z, cw_ref):
    conv = cw_ref[0:1, :] * u_m2 + cw_ref[1:2, :] * u_m1 + cw_ref[2:3, :] * u
    return (b_g * conv) * (z * jax.nn.sigmoid(z))


def _conv_proj_long_kernel(x_ref, wb_ref, wc_ref, wh_ref, wz_ref, cw_ref, prev_ref, *rest,
                           tiles_per_seq, n_casts):
    cast_src = rest[:n_casts]
    y_ref, tail_ref = rest[n_casts:n_casts + 2]
    cast_dst = rest[n_casts + 2:2 * n_casts + 2]
    carry_ref = rest[2 * n_casts + 2]
    _run_casts(cast_src, cast_dst)
    m = pl.program_id(1)

    @pl.when(lax.rem(m, tiles_per_seq) == 0)
    def _():
        carry_ref[...] = prev_ref[0]

    x = x_ref[...]
    u = _dot(x, wc_ref[...]) * _dot(x, wh_ref[...])
    tm = u.shape[0]
    row = lax.broadcasted_iota(jnp.int32, u.shape, 0)
    c_m2 = carry_ref[SUBLANES - 2:SUBLANES - 1, :]
    c_m1 = carry_ref[SUBLANES - 1:SUBLANES, :]
    u_m1 = jnp.where(row == 0, c_m1, pltpu.roll(u, 1, 0))
    u_m2 = jnp.where(row == 0, c_m2, jnp.where(row == 1, c_m1, pltpu.roll(u, 2, 0)))
    y = _gated_conv(_dot(x, wb_ref[...]), u, u_m1, u_m2, _dot(x, wz_ref[...]), cw_ref)
    y_ref[...] = y.astype(y_ref.dtype)
    tail = u[tm - SUBLANES:, :]
    carry_ref[...] = tail
    tail_ref[0] = tail


def _conv_proj_short_kernel(x_ref, wb_ref, wc_ref, wh_ref, wz_ref, cw_ref, pm1_ref, pm2_ref,
                            y_ref, u_ref, *, seq_len):
    x = x_ref[...]
    u = _dot(x, wc_ref[...]) * _dot(x, wh_ref[...])
    pos = lax.rem(lax.broadcasted_iota(jnp.int32, u.shape, 0), seq_len)
    u_m1 = jnp.where(pos == 0, pm1_ref[...], pltpu.roll(u, 1, 0))
    u_m2 = jnp.where(pos < 2, pm2_ref[...], pltpu.roll(u, 2, 0))
    y = _gated_conv(_dot(x, wb_ref[...]), u, u_m1, u_m2, _dot(x, wz_ref[...]), cw_ref)
    y_ref[...] = y.astype(y_ref.dtype)
    u_ref[...] = u


def _conv_proj(x16, w_in16, layer, conv_w, conv_prev, seq_len, casts=()):
    m_rows, d = x16.shape
    n_seq = m_rows // seq_len
    tn = 256
    nj = d // tn
    w_specs = [pl.BlockSpec((None, d, tn),
                            functools.partial(lambda j, m, g: (layer, 0, g * nj + j), g=g))
               for g in range(4)]
    cw_spec = pl.BlockSpec((CONV_W, tn), lambda j, m: (0, j))
    if seq_len >= 512:
        tm = 1024 if seq_len % 1024 == 0 else 512
        tiles_per_seq = seq_len // tm
        nm = m_rows // tm
        prev8 = jnp.pad(conv_prev.astype(F32), ((0, 0), (SUBLANES - (CONV_W - 1), 0), (0, 0)))
        c_in, c_out, c_shapes, c_ops = _cast_jobs(casts, nj * nm, lambda j, m: j * nm + m)
        y, tail, *cast16 = pl.pallas_call(
            functools.partial(_conv_proj_long_kernel, tiles_per_seq=tiles_per_seq, n_casts=len(casts)),
            out_shape=(jax.ShapeDtypeStruct((m_rows, d), BF16),
                       jax.ShapeDtypeStruct((n_seq, SUBLANES, d), F32), *c_shapes),
            grid=(nj, nm),
            in_specs=[pl.BlockSpec((tm, d), lambda j, m: (m, 0))] + w_specs + [
                cw_spec,
                pl.BlockSpec((1, SUBLANES, tn), lambda j, m: (m // tiles_per_seq, 0, j))] + c_in,
            out_specs=(pl.BlockSpec((tm, tn), lambda j, m: (m, j)),
                       pl.BlockSpec((1, SUBLANES, tn), lambda j, m: (m // tiles_per_seq, 0, j)),
                       *c_out),
            scratch_shapes=[pltpu.VMEM((SUBLANES, tn), F32)],
            compiler_params=_params("arbitrary", "arbitrary"),
            name="conv_proj_long",
        )(x16, w_in16, w_in16, w_in16, w_in16, conv_w, prev8, *c_ops)
        return y, tail[:, SUBLANES - (CONV_W - 1):, :], cast16

    assert not casts
    prev = conv_prev.astype(F32)
    zeros = jnp.zeros((n_seq, seq_len - 1, d), F32)
    pm1 = jnp.concatenate([prev[:, 1:2], zeros], axis=1).reshape(m_rows, d)
    pm2 = jnp.concatenate([prev, zeros[:, 1:]], axis=1).reshape(m_rows, d)
    y, u = pl.pallas_call(
        functools.partial(_conv_proj_short_kernel, seq_len=seq_len),
        out_shape=(jax.ShapeDtypeStruct((m_rows, d), BF16),
                   jax.ShapeDtypeStruct((m_rows, d), F32)),
        grid=(nj, 1),
        in_specs=[pl.BlockSpec((m_rows, d), lambda j, m: (0, 0))] + w_specs + [
            cw_spec,
            pl.BlockSpec((m_rows, tn), lambda j, m: (0, j)),
            pl.BlockSpec((m_rows, tn), lambda j, m: (0, j))],
        out_specs=(pl.BlockSpec((m_rows, tn), lambda j, m: (0, j)),
                   pl.BlockSpec((m_rows, tn), lambda j, m: (0, j))),
        compiler_params=_params("arbitrary", "arbitrary"),
        name="conv_proj_short",
    )(x16, w_in16, w_in16, w_in16, w_in16, conv_w, pm1, pm2)
    return y, u.reshape(n_seq, seq_len, d)[:, seq_len - (CONV_W - 1):, :], []


def _fold_lanes(v):
    out = v[:, :LANES]
    for t in range(1, v.shape[1] // LANES):
        out = out + v[:, t * LANES:(t + 1) * LANES]
    return out


def _out_proj_ln_kernel(y_ref, w_ref, xres_ref, g_ref, b_ref, xo_ref, xo16_ref,
                        r_ref, shift_ref, sum_ref, sq_ref, mean_ref, rstd_ref, *, nm, nj, d):
    mm = pl.program_id(0)
    j = pl.program_id(1)

    @pl.when(mm > 0)
    def _():
        o = (r_ref[j] - mean_ref[:, 0:1]) * rstd_ref[:, 0:1] * g_ref[...] + b_ref[...]
        xo_ref[...] = o
        xo16_ref[...] = o.astype(xo16_ref.dtype)

    @pl.when(mm < nm)
    def _():
        r = ALPHA * xres_ref[...] + _dot(y_ref[...], w_ref[...])
        r_ref[j] = r

        @pl.when(j == 0)
        def _():
            shift_ref[...] = jnp.broadcast_to(jnp.mean(r, axis=-1, keepdims=True), shift_ref.shape)
            sum_ref[...] = jnp.zeros_like(sum_ref)
            sq_ref[...] = jnp.zeros_like(sq_ref)

        dlt = r - shift_ref[:, 0:1]
        sum_ref[...] += _fold_lanes(dlt)
        sq_ref[...] += _fold_lanes(dlt * dlt)

        @pl.when(j == nj - 1)
        def _():
            e1 = jnp.sum(sum_ref[...], axis=-1, keepdims=True) / d
            e2 = jnp.sum(sq_ref[...], axis=-1, keepdims=True) / d
            var = jnp.maximum(e2 - e1 * e1, 0.0)
            mean_ref[...] = jnp.broadcast_to(shift_ref[:, 0:1] + e1, mean_ref.shape)
            rstd_ref[...] = jnp.broadcast_to(lax.rsqrt(var + LN_EPS), rstd_ref.shape)


def _out_proj_ln(y16, w16, layer, x_res, gain, bias):
    m_rows, k = y16.shape
    d = w16.shape[2]
    tm = min(1024, m_rows)
    tn = 512
    assert m_rows % tm == 0 and d % tn == 0
    nj = d // tn
    nm = m_rows // tm
    row_in = lambda mm: jnp.minimum(mm, nm - 1)
    col_in = lambda mm, j: jnp.where(mm < nm, j, nj - 1)
    out_idx = lambda mm, j: (jnp.maximum(mm - 1, 0), jnp.where(mm == 0, 0, j))
    stat = pltpu.VMEM((tm, LANES), F32)
    return pl.pallas_call(
        functools.partial(_out_proj_ln_kernel, nm=nm, nj=nj, d=d),
        out_shape=(jax.ShapeDtypeStruct((m_rows, d), F32), jax.ShapeDtypeStruct((m_rows, d), BF16)),
        grid=(nm + 1, nj),
        in_specs=[pl.BlockSpec((tm, k), lambda mm, j: (row_in(mm), 0)),
                  pl.BlockSpec((None, k, tn), lambda mm, j: (layer, 0, col_in(mm, j))),
                  pl.BlockSpec((tm, tn), lambda mm, j: (row_in(mm), col_in(mm, j))),
                  pl.BlockSpec((1, tn), lambda mm, j: (0, j)),
                  pl.BlockSpec((1, tn), lambda mm, j: (0, j))],
        out_specs=(pl.BlockSpec((tm, tn), out_idx), pl.BlockSpec((tm, tn), out_idx)),
        scratch_shapes=[pltpu.VMEM((nj, tm, tn), F32), stat, stat, stat, stat, stat],
        compiler_params=_params("arbitrary", "arbitrary"),
        name="out_proj_ln",
    )(y16, w16, x_res, gain.reshape(1, d), bias.reshape(1, d))


def _dual_proj_kernel(x_ref, wa_ref, wb_ref, *rest, n_casts):
    cast_src = rest[:n_casts]
    oa_ref, ob_ref = rest[n_casts:n_casts + 2]
    _run_casts(cast_src, rest[n_casts + 2:])
    x = x_ref[...]
    oa_ref[...] = _dot(x, wa_ref[...])
    ob_ref[...] = _dot(x, wb_ref[...])


def _dual_proj(x16, w16, layer, casts=()):
    m_rows, k = x16.shape
    n_half = w16.shape[2] // 2
    tm = next(t for t in (1024, 512, m_rows) if m_rows % t == 0)
    tn = 512
    nj = n_half // tn
    out = jax.ShapeDtypeStruct((m_rows, n_half), F32)
    nm = m_rows // tm
    c_in, c_out, c_shapes, c_ops = _cast_jobs(casts, nm * nj, lambda m, j: m * nj + j)
    first, second, *cast16 = pl.pallas_call(
        functools.partial(_dual_proj_kernel, n_casts=len(casts)),
        out_shape=(out, out, *c_shapes),
        grid=(nm, nj),
        in_specs=[pl.BlockSpec((tm, k), lambda m, j: (m, 0)),
                  pl.BlockSpec((None, k, tn), lambda m, j: (layer, 0, j)),
                  pl.BlockSpec((None, k, tn), lambda m, j: (layer, 0, nj + j))] + c_in,
        out_specs=(pl.BlockSpec((tm, tn), lambda m, j: (m, j)),
                   pl.BlockSpec((tm, tn), lambda m, j: (m, j)), *c_out),
        compiler_params=_params("arbitrary", "arbitrary"),
        name="dual_proj",
    )(x16, w16, w16, *c_ops)
    return first, second, cast16


def _moba_prefill_kernel(q_ref, k_ref, v_ref, g_ref, kx_ref, qx_ref, o_ref,
                         qa_ref, ka_ref, vb_ref, s_ref):
    seq = q_ref.shape[0]
    nb = seq // BLOCK
    q16 = q_ref[...].astype(BF16)
    k = k_ref[...]

    kmean = jnp.mean(k.reshape(nb, BLOCK, HEAD_DIM), axis=1)
    gate_t = _dot_nt(kmean.astype(BF16), q16)
    blk = lax.broadcasted_iota(jnp.int32, (nb, seq), 0)
    q_blk = lax.broadcasted_iota(jnp.int32, (nb, seq), 1) // BLOCK
    rank = jnp.zeros((nb, seq), jnp.int32)
    for other in range(nb):
        g_o = gate_t[other:other + 1, :]
        beats = (g_o > gate_t) | ((g_o == gate_t) & (other < blk))
        rank += jnp.where((q_blk > other) & beats, 1, 0)
    dropped = ((blk < q_blk) & (rank >= TOP_K)).astype(F32)
    lane = lax.broadcasted_iota(jnp.int32, (nb, LANES), 1)
    spread = jnp.where(lane == lax.broadcasted_iota(jnp.int32, (nb, LANES), 0), -MASK_BIG, 0.0)
    sel_bias = lax.dot_general(dropped, spread, (((0,), (0,)), ((), ())),
                               preferred_element_type=F32)

    qa_ref[:, :HEAD_DIM] = q16
    qa_ref[:, HEAD_DIM:] = (sel_bias + qx_ref[0, 0:1, :]).astype(BF16)
    ka_ref[:, :HEAD_DIM] = k.astype(BF16)
    ka_ref[:, HEAD_DIM:] = kx_ref[...]
    vb_ref[:, :HEAD_DIM] = v_ref[...].astype(BF16)
    ones_col = lax.broadcasted_iota(jnp.int32, (seq, HEAD_DIM), 1) == 0
    vb_ref[:, HEAD_DIM:] = jnp.where(ones_col, 1.0, 0.0).astype(BF16)

    r_i = lax.broadcasted_iota(jnp.int32, (BLOCK, BLOCK), 0)
    c_i = lax.broadcasted_iota(jnp.int32, (BLOCK, BLOCK), 1)
    causal = jnp.where(c_i <= r_i, 0.0, -jnp.inf)
    for i in range(nb):
        s_ref[i, i * BLOCK:, :] = _dot_nt(qa_ref[i * BLOCK:, :], ka_ref[i * BLOCK:(i + 1) * BLOCK, :])
        rows = slice(i * BLOCK, (i + 1) * BLOCK)
        n_keys = (i + 1) * BLOCK
        parts = [s_ref[n, rows, :] for n in range(i)] + [s_ref[i, rows, :] + causal]
        s = parts[0] if i == 0 else jnp.concatenate(parts, axis=1)
        m = jnp.max(s, axis=-1, keepdims=True)
        p = jnp.exp2((s - m) * (SCALE * LOG2_E))
        ol = _dot(p.astype(BF16), vb_ref[0:n_keys, :])
        o = ol[:, :HEAD_DIM] / ol[:, HEAD_DIM:HEAD_DIM + 1]
        gate = g_ref[rows, :]
        o_ref[rows, :] = (o * (gate * jax.nn.sigmoid(gate))).astype(o_ref.dtype)


def _alibi_slopes():
    return 2.0 ** (-8.0 * jnp.arange(1, N_HEADS + 1, dtype=F32) / N_HEADS)


def _moba_prefill(q, k, v, g, n_batch, seq):
    nb = seq // BLOCK
    assert nb <= SUBLANES and seq % BLOCK == 0
    pos = np.arange(seq)
    kx = np.zeros((seq, LANES), np.float32)
    kx[pos, pos // BLOCK] = 1.0
    kx[:, SUBLANES:SUBLANES + ALIBI_PARTS] = (pos % BLOCK)[:, None]
    kx[:, SUBLANES + ALIBI_PARTS:SUBLANES + 2 * ALIBI_PARTS] = ((pos // BLOCK) * BLOCK)[:, None]
    rest = _alibi_slopes() / SCALE
    pieces = []
    for _ in range(ALIBI_PARTS):
        piece = rest.astype(BF16).astype(F32)
        pieces.append(piece)
        rest = rest - piece
    pieces = jnp.stack(pieces, axis=1)
    qx = jnp.zeros((N_HEADS, LANES), F32)
    qx = qx.at[:, SUBLANES:SUBLANES + ALIBI_PARTS].set(pieces)
    qx = qx.at[:, SUBLANES + ALIBI_PARTS:SUBLANES + 2 * ALIBI_PARTS].set(pieces)
    qx = jnp.broadcast_to(qx[:, None, :], (N_HEADS, SUBLANES, LANES))
    head_spec = pl.BlockSpec((seq, HEAD_DIM), lambda b, h: (b, h))
    return pl.pallas_call(
        _moba_prefill_kernel,
        out_shape=jax.ShapeDtypeStruct(q.shape, BF16),
        grid=(n_batch, N_HEADS),
        in_specs=[head_spec, head_spec, head_spec, head_spec,
                  pl.BlockSpec((seq, LANES), lambda b, h: (0, 0)),
                  pl.BlockSpec((1, SUBLANES, LANES), lambda b, h: (h, 0, 0))],
        out_specs=head_spec,
        scratch_shapes=[pltpu.VMEM((seq, 2 * HEAD_DIM), BF16),
                        pltpu.VMEM((seq, 2 * HEAD_DIM), BF16),
                        pltpu.VMEM((seq, 2 * HEAD_DIM), BF16),
                        pltpu.VMEM((nb, seq, BLOCK), F32)],
        compiler_params=_params("arbitrary", "arbitrary"),
        name="moba_prefill",
    )(q, k, v, g, jnp.asarray(kx, BF16), qx)


def _block_diag_rows(x, n_q):
    rows = N_HEADS * n_q
    tiled = jnp.concatenate([x] * N_HEADS, axis=0)
    r_h = lax.broadcasted_iota(jnp.int32, (rows, x.shape[1]), 0) // n_q
    c_h = lax.broadcasted_iota(jnp.int32, (rows, x.shape[1]), 1) // HEAD_DIM
    return jnp.where(r_h == c_h, tiled, 0.0)


def _pad_rows(x, rows):
    return jnp.concatenate([x, jnp.zeros((rows - x.shape[0], x.shape[1]), x.dtype)], axis=0)


def _gather_heads(page_refs, dst_ref):
    for i in range(len(page_refs) // HEAD_GROUPS):
        for hg in range(HEAD_GROUPS):
            flat = page_refs[i * HEAD_GROUPS + hg].reshape(PAGE_SIZE * SUBLANES, HEAD_DIM)
            for s in range(SUBLANES):
                h = hg * SUBLANES + s
                dst_ref[i * PAGE_SIZE:(i + 1) * PAGE_SIZE, h * HEAD_DIM:(h + 1) * HEAD_DIM] = (
                    flat[pl.ds(s, PAGE_SIZE, stride=SUBLANES), :].astype(BF16))


def _moba_decode_scores_kernel(pt_ref, q_ref, knew_ref, slope_ref, *rest, n_q, past_len):
    n_in = SCORE_PAGES_PER_STEP * HEAD_GROUPS
    pages = rest[:n_in]
    p_ref, pown_ref, l_ref, s_ref, ksum_ref, kc_ref, qbd_ref, pick_ref = rest[n_in:]
    c = pl.program_id(1)
    n_chunks = pl.num_programs(1)
    chunk = SCORE_PAGES_PER_STEP * PAGE_SIZE
    blocks_per_chunk = chunk // BLOCK
    n_blocks = past_len // BLOCK
    rows = N_HEADS * n_q

    @pl.when(c == 0)
    def _():
        qbd_ref[...] = _block_diag_rows(q_ref[0], n_q).T.astype(BF16)

    _gather_heads(pages, kc_ref)
    pages_per_block = BLOCK // PAGE_SIZE
    for bi in range(blocks_per_chunk):
        for hg in range(HEAD_GROUPS):
            tot = None
            for i in range(bi * pages_per_block, (bi + 1) * pages_per_block):
                part = jnp.sum(pages[i * HEAD_GROUPS + hg][...], axis=0)
                tot = part if tot is None else tot + part
            row = ((c * blocks_per_chunk + bi) * HEAD_GROUPS + hg) * SUBLANES
            ksum_ref[pl.ds(pl.multiple_of(row, SUBLANES), SUBLANES), :] = tot
    s_ref[c] = _dot(kc_ref[...], qbd_ref[...])

    @pl.when(c == n_chunks - 1)
    def _():
        qbd = qbd_ref[...]
        ksum = jnp.concatenate([ksum_ref[pl.ds(h, n_blocks, stride=N_HEADS), :]
                                for h in range(N_HEADS)], axis=1)
        kmean = (ksum / BLOCK).astype(BF16)
        gate = _dot(kmean, qbd)
        idx = lax.broadcasted_iota(jnp.int32, gate.shape, 0)
        picked = jnp.zeros(gate.shape, jnp.bool_)
        for _ in range(TOP_K):
            best = jnp.max(gate, axis=0, keepdims=True)
            first = jnp.min(jnp.where(gate == best, idx, n_blocks), axis=0, keepdims=True)
            hit = idx == first
            picked = picked | hit
            gate = jnp.where(hit, -jnp.inf, gate)
        pick_ref[...] = picked.astype(F32)

        slope = slope_ref[0:1, :]
        q_i = lax.rem(lax.broadcasted_iota(jnp.int32, (1, rows), 1), n_q)
        t = (past_len + q_i).astype(F32)

        def mask_chunk(ci, m_run):
            pos = (ci * chunk + lax.broadcasted_iota(jnp.int32, (chunk, 1), 0)).astype(F32)
            s = s_ref[ci] * SCALE - slope * (t - pos)
            parts = []
            for bi in range(blocks_per_chunk):
                keep = pick_ref[pl.ds(ci * blocks_per_chunk + bi, 1), :]
                parts.append(jnp.where(keep > 0.5, s[bi * BLOCK:(bi + 1) * BLOCK], -jnp.inf))
            s = jnp.concatenate(parts, axis=0)
            s_ref[ci] = s
            return jnp.maximum(m_run, jnp.max(s, axis=0, keepdims=True))

        m_past = lax.fori_loop(0, n_chunks, mask_chunk, jnp.full((1, rows), -jnp.inf, F32))

        k_own = _pad_rows(knew_ref[0], LANES).astype(BF16)
        j_own = lax.broadcasted_iota(jnp.int32, (LANES, rows), 0)
        s_own = _dot(k_own, qbd) * SCALE - slope * (q_i - j_own).astype(F32)
        s_own = jnp.where(j_own <= q_i, s_own, -jnp.inf)
        m_all = jnp.maximum(m_past, jnp.max(s_own, axis=0, keepdims=True))
        p_own = jnp.exp(s_own - m_all)
        pown_ref[0] = p_own.T

        def exp_chunk(ci, l_run):
            p = jnp.exp(s_ref[ci] - m_all)
            p_ref[0, ci] = p.T.astype(p_ref.dtype)
            return l_run + jnp.sum(p, axis=0, keepdims=True)

        l_all = lax.fori_loop(0, n_chunks, exp_chunk, jnp.sum(p_own, axis=0, keepdims=True))
        l_ref[0] = jnp.broadcast_to(l_all, (LANES, rows)).T


def _moba_decode_values_kernel(pt_ref, p_ref, pown_ref, l_ref, vnew_ref, g_ref, *rest, n_q):
    n_in = VALUE_PAGES_PER_STEP * HEAD_GROUPS
    pages = rest[:n_in]
    o_ref, acc_ref, vc_ref = rest[n_in:]
    c = pl.program_id(1)
    score_chunk = SCORE_PAGES_PER_STEP * PAGE_SIZE

    _gather_heads(pages, vc_ref)
    part = None
    for sub in range(VALUE_PAGES_PER_STEP // SCORE_PAGES_PER_STEP):
        term = _dot(p_ref[0, sub], vc_ref[sub * score_chunk:(sub + 1) * score_chunk, :])
        part = term if part is None else part + term

    @pl.when(c == 0)
    def _():
        acc_ref[...] = part

    @pl.when(c > 0)
    def _():
        acc_ref[...] += part

    @pl.when(c == pl.num_programs(1) - 1)
    def _():
        v_own = _pad_rows(vnew_ref[0], LANES).astype(BF16)
        full = (acc_ref[...] + _dot(pown_ref[0].astype(BF16), v_own)) / l_ref[0][:, 0:1]
        r_h = lax.broadcasted_iota(jnp.int32, full.shape, 0) // n_q
        c_h = lax.broadcasted_iota(jnp.int32, full.shape, 1) // HEAD_DIM
        o = jnp.sum(jnp.where(r_h == c_h, full, 0.0).reshape(N_HEADS, n_q, full.shape[1]), axis=0)
        gate = g_ref[0]
        o_ref[0] = o * (gate * jax.nn.sigmoid(gate))


def _moba_decode(q, g, k_new, v_new, cache_k, cache_v, page_table, n_batch, n_q):
    d = q.shape[1]
    n_pages = page_table.shape[1]
    past_len = n_pages * PAGE_SIZE
    assert n_q == SUBLANES and past_len % BLOCK == 0 and n_pages % VALUE_PAGES_PER_STEP == 0
    assert VALUE_PAGES_PER_STEP % SCORE_PAGES_PER_STEP == 0
    n_chunks = n_pages // SCORE_PAGES_PER_STEP
    chunk = SCORE_PAGES_PER_STEP * PAGE_SIZE
    chunks_per_value_step = VALUE_PAGES_PER_STEP // SCORE_PAGES_PER_STEP
    n_blocks = past_len // BLOCK
    rows = N_HEADS * n_q
    pt = page_table.reshape(-1).astype(jnp.int32)
    ck = cache_k.reshape(cache_k.shape[0] * PAGE_SIZE, N_HEADS, HEAD_DIM)
    cv = cache_v.reshape(cache_v.shape[0] * PAGE_SIZE, N_HEADS, HEAD_DIM)
    q3, g3 = q.reshape(n_batch, n_q, d), g.reshape(n_batch, n_q, d)
    kn3, vn3 = k_new.reshape(n_batch, n_q, d), v_new.reshape(n_batch, n_q, d)
    slope_cols = jnp.broadcast_to(jnp.repeat(_alibi_slopes(), n_q)[None, :], (SUBLANES, rows))

    def page_specs(pages_per_step):
        def one(i, hg):
            return pl.BlockSpec(
                (PAGE_SIZE, SUBLANES, HEAD_DIM),
                lambda b, c, pt_ref: (pt_ref[b * n_pages + c * pages_per_step + i], hg, 0))
        return [one(i, hg) for i in range(pages_per_step) for hg in range(HEAD_GROUPS)]

    tok_spec = pl.BlockSpec((1, n_q, d), lambda b, c, pt_ref: (b, 0, 0))
    row_spec = pl.BlockSpec((1, rows, LANES), lambda b, c, pt_ref: (b, 0, 0))
    k_specs = page_specs(SCORE_PAGES_PER_STEP)
    probs, p_own, denom = pl.pallas_call(
        functools.partial(_moba_decode_scores_kernel, n_q=n_q, past_len=past_len),
        out_shape=(jax.ShapeDtypeStruct((n_batch, n_chunks, rows, chunk), BF16),
                   jax.ShapeDtypeStruct((n_batch, rows, LANES), F32),
                   jax.ShapeDtypeStruct((n_batch, rows, LANES), F32)),
        grid_spec=pltpu.PrefetchScalarGridSpec(
            num_scalar_prefetch=1,
            grid=(n_batch, n_chunks),
            in_specs=[tok_spec, tok_spec,
                      pl.BlockSpec((SUBLANES, rows), lambda b, c, pt_ref: (0, 0))] + k_specs,
            out_specs=(pl.BlockSpec((1, n_chunks, rows, chunk), lambda b, c, pt_ref: (b, 0, 0, 0)),
                       row_spec, row_spec),
            scratch_shapes=[pltpu.VMEM((n_chunks, chunk, rows), F32),
                            pltpu.VMEM((n_blocks * N_HEADS, HEAD_DIM), F32),
                            pltpu.VMEM((chunk, d), BF16),
                            pltpu.VMEM((d, rows), BF16),
                            pltpu.VMEM((n_blocks, rows), F32)]),
        compiler_params=_params("arbitrary", "arbitrary"),
        name="moba_decode_scores",
    )(pt, q3, kn3, slope_cols, *([ck] * len(k_specs)))

    v_specs = page_specs(VALUE_PAGES_PER_STEP)
    out = pl.pallas_call(
        functools.partial(_moba_decode_values_kernel, n_q=n_q),
        out_shape=jax.ShapeDtypeStruct((n_batch, n_q, d), F32),
        grid_spec=pltpu.PrefetchScalarGridSpec(
            num_scalar_prefetch=1,
            grid=(n_batch, n_pages // VALUE_PAGES_PER_STEP),
            in_specs=[pl.BlockSpec((1, chunks_per_value_step, rows, chunk),
                                   lambda b, c, pt_ref: (b, c, 0, 0)),
                      row_spec, row_spec, tok_spec, tok_spec] + v_specs,
            out_specs=tok_spec,
            scratch_shapes=[pltpu.VMEM((rows, d), F32),
                            pltpu.VMEM((VALUE_PAGES_PER_STEP * PAGE_SIZE, d), BF16)]),
        compiler_params=_params("arbitrary", "arbitrary"),
        name="moba_decode_values",
    )(pt, probs, p_own, denom, vn3, g3, *([cv] * len(v_specs)))
    return out.reshape(n_batch * n_q, d).astype(BF16)


def _trunk(x, conv_prev, params, w16, attention):
    conv_w, ln_g, ln_b = params["conv_w"], params["ln_g"], params["ln_b"]
    n_seq, t_len, d = x.shape
    x32 = x.reshape(n_seq * t_len, d)
    x16 = x32.astype(BF16)

    def plan(*names):
        todo = [(n, l) for n, l in names if w16[n][l] is None]
        w3 = lambda n: params[n] if params[n].ndim == 3 else params[n][None]
        return todo, [(w3(n), l) for n, l in todo]

    def keep(todo, copies):
        for (n, l), c in zip(todo, copies):
            w16[n][l] = c

    conv_rows = []
    ahead = [
        [("w_out_a", 0), ("w_in_a", 1)],
        [("w_out_a", 1), ("w_kv", 0), ("w_in_b", 0)],
    ]
    for l in range(N_A):
        todo, jobs = plan(*ahead[l])
        y16, rows, copies = _conv_proj(x16, w16["w_in_a"][l], 0, conv_w[l], conv_prev[l], t_len, jobs)
        keep(todo, copies)
        conv_rows.append(rows)
        x32, x16 = _out_proj_ln(y16, w16["w_out_a"][l], 0, x32, ln_g[l], ln_b[l])
    todo, jobs = plan(("w_in_b", 1))
    k_new, v_new, copies = _dual_proj(x16, w16["w_kv"][0], 0, jobs)
    keep(todo, copies)
    for l in range(N_A, DEPTH):
        todo, jobs = plan(("w_out_b", 0), ("w_out_b", 1)) if l == N_A else ([], [])
        q, g, copies = _dual_proj(x16, w16["w_in_b"][l - N_A], 0, jobs)
        keep(todo, copies)
        o16 = attention(q, g, k_new, v_new)
        x32, x16 = _out_proj_ln(o16, w16["w_out_b"][l - N_A], 0, x32, ln_g[l], ln_b[l])
    heads = (n_seq, t_len, N_HEADS, HEAD_DIM)
    return (x32.reshape(n_seq, t_len, d), k_new.reshape(heads), v_new.reshape(heads),
            jnp.stack(conv_rows))


def kernel(x_prompt, x_sample, cache_k, cache_v, state_conv, page_table, w_in_a, conv_w, w_out_a,
           w_kv, w_in_b, w_out_b, ln_g, ln_b):
    params = dict(w_in_a=w_in_a, conv_w=conv_w, w_out_a=w_out_a, w_kv=w_kv, w_in_b=w_in_b,
                  w_out_b=w_out_b, ln_g=ln_g, ln_b=ln_b)
    w16 = dict(w_in_a=[w_in_a[0:1].astype(BF16), None], w_out_a=[None, None], w_kv=[None],
               w_in_b=[None, None], w_out_b=[None, None])
    n_prompt, seq, d = x_prompt.shape
    n_dec, dec_seq, _ = x_sample.shape

    zero_conv = jnp.zeros((N_A, n_prompt, CONV_W - 1, d), x_prompt.dtype)
    y_prompt, k_prompt, v_prompt, conv_prompt = _trunk(
        x_prompt, zero_conv, params, w16,
        lambda q, g, k, v: _moba_prefill(q, k, v, g, n_prompt, seq))
    y_sample, k_sample, v_sample, conv_sample = _trunk(
        x_sample, state_conv, params, w16,
        lambda q, g, k, v: _moba_decode(q, g, k, v, cache_k, cache_v, page_table, n_dec, dec_seq))
    return (y_prompt, y_sample, k_prompt, v_prompt, conv_prompt, k_sample, v_sample, conv_sample)
```

```python
import functools
import math

import numpy as np
import jax
import jax.numpy as jnp
from jax import lax
from jax.experimental import pallas as pl
from jax.experimental.pallas import tpu as pltpu

N_HEADS = 32
HEAD_DIM = 128
CONV_W = 3
BLOCK = 256
TOP_K = 3
PAGE_SIZE = 128
DEPTH = 4
N_A = DEPTH // 2
ALPHA = (2.0 * DEPTH) ** 0.25
LN_EPS = 1e-5
SCALE = 1.0 / math.sqrt(HEAD_DIM)
LOG2_E = math.log2(math.e)

SUBLANES = 8
LANES = 128
VMEM_LIMIT_BYTES = 56 * 1024 * 1024
LN_VMEM_LIMIT_BYTES = 58 * 1024 * 1024
MASK_BIG = 2.0 ** 100
SCORE_PAGES_PER_STEP = 4
VALUE_PAGES_PER_STEP = 8
HEAD_GROUPS = N_HEADS // SUBLANES
ALIBI_PARTS = 4

F32 = jnp.float32
BF16 = jnp.bfloat16


def _params(*semantics, vmem_limit=VMEM_LIMIT_BYTES):
    return pltpu.CompilerParams(dimension_semantics=semantics, vmem_limit_bytes=vmem_limit)


def _dot(a, b):
    return jnp.dot(a, b, preferred_element_type=F32)


def _dot_nt(a, b):
    return lax.dot_general(a, b, (((1,), (1,)), ((), ())), preferred_element_type=F32)


def _cast_jobs(jobs, n_steps, step_of):
    in_specs, out_specs, out_shapes, operands = [], [], [], []
    for w, layer in jobs:
        _, r, c = w.shape
        assert r % n_steps == 0 and (r // n_steps) % 16 == 0
        slab = r // n_steps
        in_specs.append(pl.BlockSpec(
            (None, slab, c), functools.partial(lambda *ids, layer: (layer, step_of(*ids), 0), layer=layer)))
        out_specs.append(pl.BlockSpec((None, slab, c), lambda *ids: (0, step_of(*ids), 0)))
        out_shapes.append(jax.ShapeDtypeStruct((1, r, c), BF16))
        operands.append(w)
    return in_specs, out_specs, out_shapes, operands


def _run_casts(src_refs, dst_refs):
    for src, dst in zip(src_refs, dst_refs):
        dst[...] = src[...].astype(BF16)


def _gated_conv(b_g, u, u_m1, u_m2, z, cw_ref):
    conv = cw_ref[0:1, :] * u_m2 + cw_ref[1:2, :] * u_m1 + cw_ref[2:3, :] * u
    return (b_g * conv) * (z * jax.nn.sigmoid(z))


def _conv_rows(x, w_refs, cw_ref, u_prev):
    wb_ref, wc_ref, wh_ref, wz_ref = w_refs
    u = _dot(x, wc_ref[...]) * _dot(x, wh_ref[...])
    u_m1, u_m2 = u_prev(u)
    return _gated_conv(_dot(x, wb_ref[...]), u, u_m1, u_m2, _dot(x, wz_ref[...]), cw_ref), u


def _conv_proj_kernel(x_ref, wb_ref, wc_ref, wh_ref, wz_ref, cw_ref, prev_ref, *rest,
                      tiles_per_seq, n_casts, short_len):
    n_short = 3 if short_len else 0
    short_in = rest[:n_short]
    cast_src = rest[n_short:n_short + n_casts]
    outs = rest[n_short + n_casts:]
    y_ref, tail_ref = outs[:2]
    short_out = outs[2:2 + (2 if short_len else 0)]
    cast_dst = outs[2 + len(short_out):2 + len(short_out) + n_casts]
    carry_ref = outs[-1]
    w_refs = (wb_ref, wc_ref, wh_ref, wz_ref)
    _run_casts(cast_src, cast_dst)
    m = pl.program_id(1)

    @pl.when(lax.rem(m, tiles_per_seq) == 0)
    def _():
        carry_ref[...] = prev_ref[0]

    def carried(u):
        row = lax.broadcasted_iota(jnp.int32, u.shape, 0)
        c_m2 = carry_ref[SUBLANES - 2:SUBLANES - 1, :]
        c_m1 = carry_ref[SUBLANES - 1:SUBLANES, :]
        u_m1 = jnp.where(row == 0, c_m1, pltpu.roll(u, 1, 0))
        u_m2 = jnp.where(row == 0, c_m2, jnp.where(row == 1, c_m1, pltpu.roll(u, 2, 0)))
        return u_m1, u_m2

    y, u = _conv_rows(x_ref[...], w_refs, cw_ref, carried)
    y_ref[...] = y.astype(y_ref.dtype)
    tail = u[u.shape[0] - SUBLANES:, :]
    carry_ref[...] = tail
    tail_ref[0] = tail

    if short_len:
        xs_ref, pm1_ref, pm2_ref = short_in
        ys_ref, us_ref = short_out

        @pl.when(m == 0)
        def _():
            def preceded(u_s):
                pos = lax.rem(lax.broadcasted_iota(jnp.int32, u_s.shape, 0), short_len)
                return (jnp.where(pos == 0, pm1_ref[...], pltpu.roll(u_s, 1, 0)),
                        jnp.where(pos < 2, pm2_ref[...], pltpu.roll(u_s, 2, 0)))

            y_s, u_s = _conv_rows(xs_ref[...], w_refs, cw_ref, preceded)
            ys_ref[...] = y_s.astype(ys_ref.dtype)
            us_ref[...] = u_s


def _conv_proj(x16, w_in16, layer, conv_w, conv_prev, seq_len, casts=(), short=None):
    m_rows, d = x16.shape
    n_seq = m_rows // seq_len
    tn = 256
    nj = d // tn
    w_specs = [pl.BlockSpec((None, d, tn),
                            functools.partial(lambda j, m, g: (layer, 0, g * nj + j), g=g))
               for g in range(4)]
    tm = 1024 if seq_len % 1024 == 0 else 512
    assert seq_len % tm == 0
    tiles_per_seq = seq_len // tm
    nm = m_rows // tm
    prev8 = jnp.pad(conv_prev.astype(F32), ((0, 0), (SUBLANES - (CONV_W - 1), 0), (0, 0)))
    c_in, c_out, c_shapes, c_ops = _cast_jobs(casts, nj * nm, lambda j, m: j * nm + m)
    s_in, s_out, s_shapes, s_ops, short_len = [], [], [], [], 0
    if short is not None:
        xs16, prev_s, short_len = short
        ms = xs16.shape[0]
        n_s = ms // short_len
        assert short_len >= CONV_W - 1 and ms % 16 == 0
        prev_s = prev_s.astype(F32)
        zeros = jnp.zeros((n_s, short_len - 1, d), F32)
        pm1 = jnp.concatenate([prev_s[:, 1:2], zeros], axis=1).reshape(ms, d)
        pm2 = jnp.concatenate([prev_s, zeros[:, 1:]], axis=1).reshape(ms, d)
        col = pl.BlockSpec((ms, tn), lambda j, m: (0, j))
        s_in = [pl.BlockSpec((ms, d), lambda j, m: (0, 0)), col, col]
        s_out = [col, col]
        s_shapes = [jax.ShapeDtypeStruct((ms, d), BF16), jax.ShapeDtypeStruct((ms, d), F32)]
        s_ops = [xs16, pm1, pm2]
    outs = pl.pallas_call(
        functools.partial(_conv_proj_kernel, tiles_per_seq=tiles_per_seq, n_casts=len(casts),
                          short_len=short_len),
        out_shape=(jax.ShapeDtypeStruct((m_rows, d), BF16),
                   jax.ShapeDtypeStruct((n_seq, SUBLANES, d), F32), *s_shapes, *c_shapes),
        grid=(nj, nm),
        in_specs=[pl.BlockSpec((tm, d), lambda j, m: (m, 0))] + w_specs + [
            pl.BlockSpec((CONV_W, tn), lambda j, m: (0, j)),
            pl.BlockSpec((1, SUBLANES, tn), lambda j, m: (m // tiles_per_seq, 0, j))] + s_in + c_in,
        out_specs=(pl.BlockSpec((tm, tn), lambda j, m: (m, j)),
                   pl.BlockSpec((1, SUBLANES, tn), lambda j, m: (m // tiles_per_seq, 0, j)),
                   *s_out, *c_out),
        scratch_shapes=[pltpu.VMEM((SUBLANES, tn), F32)],
        compiler_params=_params("arbitrary", "arbitrary"),
        name="conv_proj",
    )(x16, w_in16, w_in16, w_in16, w_in16, conv_w, prev8, *s_ops, *c_ops)
    y, tail = outs[:2]
    short_res = None
    if short is not None:
        y_s, u_s = outs[2:4]
        short_res = (y_s, u_s.reshape(n_s, short_len, d)[:, short_len - (CONV_W - 1):, :])
    return y, tail[:, SUBLANES - (CONV_W - 1):, :], list(outs[2 + len(s_shapes):]), short_res


def _fold_lanes(v):
    out = v[:, :LANES]
    for t in range(1, v.shape[1] // LANES):
        out = out + v[:, t * LANES:(t + 1) * LANES]
    return out


def _ln_collect(r, j, nj, d, r_ref, shift_ref, sum_ref, sq_ref, mean_ref, rstd_ref):
    r_ref[j] = r

    @pl.when(j == 0)
    def _():
        shift_ref[...] = jnp.broadcast_to(jnp.mean(r, axis=-1, keepdims=True), shift_ref.shape)
        sum_ref[...] = jnp.zeros_like(sum_ref)
        sq_ref[...] = jnp.zeros_like(sq_ref)

    dlt = r - shift_ref[:, 0:1]
    sum_ref[...] += _fold_lanes(dlt)
    sq_ref[...] += _fold_lanes(dlt * dlt)

    @pl.when(j == nj - 1)
    def _():
        e1 = jnp.sum(sum_ref[...], axis=-1, keepdims=True) / d
        e2 = jnp.sum(sq_ref[...], axis=-1, keepdims=True) / d
        var = jnp.maximum(e2 - e1 * e1, 0.0)
        mean_ref[...] = jnp.broadcast_to(shift_ref[:, 0:1] + e1, mean_ref.shape)
        rstd_ref[...] = jnp.broadcast_to(lax.rsqrt(var + LN_EPS), rstd_ref.shape)


def _ln_emit(j, r_ref, mean_ref, rstd_ref, g_ref, b_ref, xo_ref, xo16_ref):
    o = (r_ref[j] - mean_ref[:, 0:1]) * rstd_ref[:, 0:1] * g_ref[...] + b_ref[...]
    xo_ref[...] = o
    xo16_ref[...] = o.astype(xo16_ref.dtype)


def _out_proj_ln_kernel(y_ref, w_ref, xres_ref, g_ref, b_ref, *rest, nm, nj, d, has_short):
    n_s = 2 if has_short else 0
    short_in = rest[:n_s]
    xo_ref, xo16_ref = rest[n_s:n_s + 2]
    short_out = rest[n_s + 2:2 * n_s + 2]
    scratch = rest[2 * n_s + 2:]
    main_state, short_state = scratch[:6], scratch[6:]
    mm = pl.program_id(0)
    j = pl.program_id(1)

    @pl.when(mm > 0)
    def _():
        _ln_emit(j, main_state[0], main_state[4], main_state[5], g_ref, b_ref, xo_ref, xo16_ref)

    @pl.when(mm < nm)
    def _():
        r = ALPHA * xres_ref[...] + _dot(y_ref[...], w_ref[...])
        _ln_collect(r, j, nj, d, *main_state)

    if has_short:
        ys_ref, xres_s_ref = short_in

        @pl.when(mm == 1)
        def _():
            _ln_emit(j, short_state[0], short_state[4], short_state[5], g_ref, b_ref, *short_out)

        @pl.when(mm == 0)
        def _():
            r_s = ALPHA * xres_s_ref[...] + _dot(ys_ref[...], w_ref[...])
            _ln_collect(r_s, j, nj, d, *short_state)


def _out_proj_ln(y16, w16, layer, x_res, gain, bias, short=None):
    m_rows, k = y16.shape
    d = w16.shape[2]
    tm = min(1024, m_rows)
    tn = 512
    assert m_rows % tm == 0 and d % tn == 0
    nj = d // tn
    nm = m_rows // tm
    row_in = lambda mm: jnp.minimum(mm, nm - 1)
    col_in = lambda mm, j: jnp.where(mm < nm, j, nj - 1)
    out_idx = lambda mm, j: (jnp.maximum(mm - 1, 0), jnp.where(mm == 0, 0, j))

    def state(rows):
        stat = pltpu.VMEM((rows, LANES), F32)
        return [pltpu.VMEM((nj, rows, tn), F32), stat, stat, stat, stat, stat]

    s_in, s_out, s_shapes, s_ops, s_state = [], [], [], [], []
    if short is not None:
        ys16, x_res_s = short
        ms = ys16.shape[0]
        s_in = [pl.BlockSpec((ms, k), lambda mm, j: (0, 0)),
                pl.BlockSpec((ms, tn), lambda mm, j: (0, jnp.where(mm == 0, j, nj - 1)))]
        s_idx = lambda mm, j: (0, jnp.where(mm == 0, 0, jnp.where(mm == 1, j, nj - 1)))
        s_out = [pl.BlockSpec((ms, tn), s_idx), pl.BlockSpec((ms, tn), s_idx)]
        s_shapes = [jax.ShapeDtypeStruct((ms, d), F32), jax.ShapeDtypeStruct((ms, d), BF16)]
        s_ops = [ys16, x_res_s]
        s_state = state(ms)
        assert nm >= 1
    return pl.pallas_call(
        functools.partial(_out_proj_ln_kernel, nm=nm, nj=nj, d=d, has_short=short is not None),
        out_shape=(jax.ShapeDtypeStruct((m_rows, d), F32), jax.ShapeDtypeStruct((m_rows, d), BF16),
                   *s_shapes),
        grid=(nm + 1, nj),
        in_specs=[pl.BlockSpec((tm, k), lambda mm, j: (row_in(mm), 0)),
                  pl.BlockSpec((None, k, tn), lambda mm, j: (layer, 0, col_in(mm, j))),
                  pl.BlockSpec((tm, tn), lambda mm, j: (row_in(mm), col_in(mm, j))),
                  pl.BlockSpec((1, tn), lambda mm, j: (0, j)),
                  pl.BlockSpec((1, tn), lambda mm, j: (0, j))] + s_in,
        out_specs=(pl.BlockSpec((tm, tn), out_idx), pl.BlockSpec((tm, tn), out_idx), *s_out),
        scratch_shapes=state(tm) + s_state,
        compiler_params=_params("arbitrary", "arbitrary", vmem_limit=LN_VMEM_LIMIT_BYTES),
        name="out_proj_ln",
    )(y16, w16, x_res, gain.reshape(1, d), bias.reshape(1, d), *s_ops)


def _dual_proj_kernel(x_ref, wa_ref, wb_ref, *rest, n_casts, has_short):
    n_s = 1 if has_short else 0
    cast_src = rest[n_s:n_s + n_casts]
    outs = rest[n_s + n_casts:]
    oa_ref, ob_ref = outs[:2]
    _run_casts(cast_src, outs[2 + 2 * n_s:])
    x = x_ref[...]
    oa_ref[...] = _dot(x, wa_ref[...])
    ob_ref[...] = _dot(x, wb_ref[...])
    if has_short:
        xs_ref = rest[0]
        oas_ref, obs_ref = outs[2:4]

        @pl.when(pl.program_id(0) == 0)
        def _():
            xs = xs_ref[...]
            oas_ref[...] = _dot(xs, wa_ref[...])
            obs_ref[...] = _dot(xs, wb_ref[...])


def _dual_proj(x16, w16, layer, casts=(), short=None):
    m_rows, k = x16.shape
    n_half = w16.shape[2] // 2
    tm = next(t for t in (1024, 512, m_rows) if m_rows % t == 0)
    tn = 512
    nj = n_half // tn
    out = jax.ShapeDtypeStruct((m_rows, n_half), F32)
    nm = m_rows // tm
    c_in, c_out, c_shapes, c_ops = _cast_jobs(casts, nm * nj, lambda m, j: m * nj + j)
    s_in, s_out, s_shapes, s_ops = [], [], [], []
    if short is not None:
        ms = short.shape[0]
        s_idx = lambda m, j: (0, jnp.where(m == 0, j, nj - 1))
        s_in = [pl.BlockSpec((ms, k), lambda m, j: (0, 0))]
        s_out = [pl.BlockSpec((ms, tn), s_idx), pl.BlockSpec((ms, tn), s_idx)]
        s_shapes = [jax.ShapeDtypeStruct((ms, n_half), F32)] * 2
        s_ops = [short]
    outs = pl.pallas_call(
        functools.partial(_dual_proj_kernel, n_casts=len(casts), has_short=short is not None),
        out_shape=(out, out, *s_shapes, *c_shapes),
        grid=(nm, nj),
        in_specs=[pl.BlockSpec((tm, k), lambda m, j: (m, 0)),
                  pl.BlockSpec((None, k, tn), lambda m, j: (layer, 0, j)),
                  pl.BlockSpec((None, k, tn), lambda m, j: (layer, 0, nj + j))] + s_in + c_in,
        out_specs=(pl.BlockSpec((tm, tn), lambda m, j: (m, j)),
                   pl.BlockSpec((tm, tn), lambda m, j: (m, j)), *s_out, *c_out),
        compiler_params=_params("arbitrary", "arbitrary"),
        name="dual_proj",
    )(x16, w16, w16, *s_ops, *c_ops)
    n_short = len(s_shapes)
    return outs[0], outs[1], list(outs[2 + n_short:]), tuple(outs[2:2 + n_short])


def _moba_prefill_kernel(q_ref, k_ref, v_ref, g_ref, kx_ref, qx_ref, o_ref,
                         qa_ref, ka_ref, vb_ref, s_ref):
    seq = q_ref.shape[0]
    nb = seq // BLOCK
    q16 = q_ref[...].astype(BF16)
    k = k_ref[...]

    kmean = jnp.mean(k.reshape(nb, BLOCK, HEAD_DIM), axis=1)
    gate_t = _dot_nt(kmean.astype(BF16), q16)
    blk = lax.broadcasted_iota(jnp.int32, (nb, seq), 0)
    q_blk = lax.broadcasted_iota(jnp.int32, (nb, seq), 1) // BLOCK
    rank = jnp.zeros((nb, seq), jnp.int32)
    for other in range(nb):
        g_o = gate_t[other:other + 1, :]
        beats = (g_o > gate_t) | ((g_o == gate_t) & (other < blk))
        rank += jnp.where((q_blk > other) & beats, 1, 0)
    dropped = ((blk < q_blk) & (rank >= TOP_K)).astype(F32)
    lane = lax.broadcasted_iota(jnp.int32, (nb, LANES), 1)
    spread = jnp.where(lane == lax.broadcasted_iota(jnp.int32, (nb, LANES), 0), -MASK_BIG, 0.0)
    sel_bias = lax.dot_general(dropped, spread, (((0,), (0,)), ((), ())),
                               preferred_element_type=F32)

    qa_ref[:, :HEAD_DIM] = q16
    qa_ref[:, HEAD_DIM:] = (sel_bias + qx_ref[0, 0:1, :]).astype(BF16)
    ka_ref[:, :HEAD_DIM] = k.astype(BF16)
    ka_ref[:, HEAD_DIM:] = kx_ref[...]
    vb_ref[:, :HEAD_DIM] = v_ref[...].astype(BF16)
    ones_col = lax.broadcasted_iota(jnp.int32, (seq, HEAD_DIM), 1) == 0
    vb_ref[:, HEAD_DIM:] = jnp.where(ones_col, 1.0, 0.0).astype(BF16)

    r_i = lax.broadcasted_iota(jnp.int32, (BLOCK, BLOCK), 0)
    c_i = lax.broadcasted_iota(jnp.int32, (BLOCK, BLOCK), 1)
    causal = jnp.where(c_i <= r_i, 0.0, -jnp.inf)
    for i in range(nb):
        s_ref[i, i * BLOCK:, :] = _dot_nt(qa_ref[i * BLOCK:, :], ka_ref[i * BLOCK:(i + 1) * BLOCK, :])
        rows = slice(i * BLOCK, (i + 1) * BLOCK)
        n_keys = (i + 1) * BLOCK
        parts = [s_ref[n, rows, :] for n in range(i)] + [s_ref[i, rows, :] + causal]
        s = parts[0] if i == 0 else jnp.concatenate(parts, axis=1)
        m = jnp.max(s, axis=-1, keepdims=True)
        p = jnp.exp2((s - m) * (SCALE * LOG2_E))
        ol = _dot(p.astype(BF16), vb_ref[0:n_keys, :])
        o = ol[:, :HEAD_DIM] / ol[:, HEAD_DIM:HEAD_DIM + 1]
        gate = g_ref[rows, :]
        o_ref[rows, :] = (o * (gate * jax.nn.sigmoid(gate))).astype(o_ref.dtype)


def _alibi_slopes():
    return 2.0 ** (-8.0 * jnp.arange(1, N_HEADS + 1, dtype=F32) / N_HEADS)


def _moba_prefill(q, k, v, g, n_batch, seq):
    nb = seq // BLOCK
    assert nb <= SUBLANES and seq % BLOCK == 0
    pos = np.arange(seq)
    kx = np.zeros((seq, LANES), np.float32)
    kx[pos, pos // BLOCK] = 1.0
    kx[:, SUBLANES:SUBLANES + ALIBI_PARTS] = (pos % BLOCK)[:, None]
    kx[:, SUBLANES + ALIBI_PARTS:SUBLANES + 2 * ALIBI_PARTS] = ((pos // BLOCK) * BLOCK)[:, None]
    rest = _alibi_slopes() / SCALE
    pieces = []
    for _ in range(ALIBI_PARTS):
        piece = rest.astype(BF16).astype(F32)
        pieces.append(piece)
        rest = rest - piece
    pieces = jnp.stack(pieces, axis=1)
    qx = jnp.zeros((N_HEADS, LANES), F32)
    qx = qx.at[:, SUBLANES:SUBLANES + ALIBI_PARTS].set(pieces)
    qx = qx.at[:, SUBLANES + ALIBI_PARTS:SUBLANES + 2 * ALIBI_PARTS].set(pieces)
    qx = jnp.broadcast_to(qx[:, None, :], (N_HEADS, SUBLANES, LANES))
    head_spec = pl.BlockSpec((seq, HEAD_DIM), lambda b, h: (b, h))
    return pl.pallas_call(
        _moba_prefill_kernel,
        out_shape=jax.ShapeDtypeStruct(q.shape, BF16),
        grid=(n_batch, N_HEADS),
        in_specs=[head_spec, head_spec, head_spec, head_spec,
                  pl.BlockSpec((seq, LANES), lambda b, h: (0, 0)),
                  pl.BlockSpec((1, SUBLANES, LANES), lambda b, h: (h, 0, 0))],
        out_specs=head_spec,
        scratch_shapes=[pltpu.VMEM((seq, 2 * HEAD_DIM), BF16),
                        pltpu.VMEM((seq, 2 * HEAD_DIM), BF16),
                        pltpu.VMEM((seq, 2 * HEAD_DIM), BF16),
                        pltpu.VMEM((nb, seq, BLOCK), F32)],
        compiler_params=_params("arbitrary", "arbitrary"),
        name="moba_prefill",
    )(q, k, v, g, jnp.asarray(kx, BF16), qx)


def _block_diag_rows(x, n_q):
    rows = N_HEADS * n_q
    tiled = jnp.concatenate([x] * N_HEADS, axis=0)
    r_h = lax.broadcasted_iota(jnp.int32, (rows, x.shape[1]), 0) // n_q
    c_h = lax.broadcasted_iota(jnp.int32, (rows, x.shape[1]), 1) // HEAD_DIM
    return jnp.where(r_h == c_h, tiled, 0.0)


def _pad_rows(x, rows):
    return jnp.concatenate([x, jnp.zeros((rows - x.shape[0], x.shape[1]), x.dtype)], axis=0)


def _gather_heads(page_refs, dst_ref):
    for i in range(len(page_refs) // HEAD_GROUPS):
        for hg in range(HEAD_GROUPS):
            flat = page_refs[i * HEAD_GROUPS + hg].reshape(PAGE_SIZE * SUBLANES, HEAD_DIM)
            for s in range(SUBLANES):
                h = hg * SUBLANES + s
                dst_ref[i * PAGE_SIZE:(i + 1) * PAGE_SIZE, h * HEAD_DIM:(h + 1) * HEAD_DIM] = (
                    flat[pl.ds(s, PAGE_SIZE, stride=SUBLANES), :].astype(BF16))


def _moba_decode_scores_kernel(pt_ref, q_ref, knew_ref, slope_ref, *rest, n_q, past_len):
    n_in = SCORE_PAGES_PER_STEP * HEAD_GROUPS
    pages = rest[:n_in]
    p_ref, pown_ref, l_ref, s_ref, ksum_ref, kc_ref, qbd_ref, pick_ref = rest[n_in:]
    c = pl.program_id(1)
    n_chunks = pl.num_programs(1)
    chunk = SCORE_PAGES_PER_STEP * PAGE_SIZE
    blocks_per_chunk = chunk // BLOCK
    n_blocks = past_len // BLOCK
    rows = N_HEADS * n_q

    @pl.when(c == 0)
    def _():
        qbd_ref[...] = _block_diag_rows(q_ref[0], n_q).T.astype(BF16)

    _gather_heads(pages, kc_ref)
    pages_per_block = BLOCK // PAGE_SIZE
    for bi in range(blocks_per_chunk):
        for hg in range(HEAD_GROUPS):
            tot = None
            for i in range(bi * pages_per_block, (bi + 1) * pages_per_block):
                part = jnp.sum(pages[i * HEAD_GROUPS + hg][...], axis=0)
                tot = part if tot is None else tot + part
            row = ((c * blocks_per_chunk + bi) * HEAD_GROUPS + hg) * SUBLANES
            ksum_ref[pl.ds(pl.multiple_of(row, SUBLANES), SUBLANES), :] = tot
    s_ref[c] = _dot(kc_ref[...], qbd_ref[...])

    @pl.when(c == n_chunks - 1)
    def _():
        qbd = qbd_ref[...]
        ksum = jnp.concatenate([ksum_ref[pl.ds(h, n_blocks, stride=N_HEADS), :]
                                for h in range(N_HEADS)], axis=1)
        kmean = (ksum / BLOCK).astype(BF16)
        gate = _dot(kmean, qbd)
        idx = lax.broadcasted_iota(jnp.int32, gate.shape, 0)
        picked = jnp.zeros(gate.shape, jnp.bool_)
        for _ in range(TOP_K):
            best = jnp.max(gate, axis=0, keepdims=True)
            first = jnp.min(jnp.where(gate == best, idx, n_blocks), axis=0, keepdims=True)
            hit = idx == first
            picked = picked | hit
            gate = jnp.where(hit, -jnp.inf, gate)
        pick_ref[...] = picked.astype(F32)

        slope = slope_ref[0:1, :]
        q_i = lax.rem(lax.broadcasted_iota(jnp.int32, (1, rows), 1), n_q)
        t = (past_len + q_i).astype(F32)

        def mask_chunk(ci, m_run):
            pos = (ci * chunk + lax.broadcasted_iota(jnp.int32, (chunk, 1), 0)).astype(F32)
            s = s_ref[ci] * SCALE - slope * (t - pos)
            parts = []
            for bi in range(blocks_per_chunk):
                keep = pick_ref[pl.ds(ci * blocks_per_chunk + bi, 1), :]
                parts.append(jnp.where(keep > 0.5, s[bi * BLOCK:(bi + 1) * BLOCK], -jnp.inf))
            s = jnp.concatenate(parts, axis=0)
            s_ref[ci] = s
            return jnp.maximum(m_run, jnp.max(s, axis=0, keepdims=True))

        m_past = lax.fori_loop(0, n_chunks, mask_chunk, jnp.full((1, rows), -jnp.inf, F32))

        k_own = _pad_rows(knew_ref[0], LANES).astype(BF16)
        j_own = lax.broadcasted_iota(jnp.int32, (LANES, rows), 0)
        s_own = _dot(k_own, qbd) * SCALE - slope * (q_i - j_own).astype(F32)
        s_own = jnp.where(j_own <= q_i, s_own, -jnp.inf)
        m_all = jnp.maximum(m_past, jnp.max(s_own, axis=0, keepdims=True))
        p_own = jnp.exp(s_own - m_all)
        pown_ref[0] = p_own.T

        def exp_chunk(ci, l_run):
            p = jnp.exp(s_ref[ci] - m_all)
            p_ref[0, ci] = p.T.astype(p_ref.dtype)
            return l_run + jnp.sum(p, axis=0, keepdims=True)

        l_all = lax.fori_loop(0, n_chunks, exp_chunk, jnp.sum(p_own, axis=0, keepdims=True))
        l_ref[0] = jnp.broadcast_to(l_all, (LANES, rows)).T


def _moba_decode_values_kernel(pt_ref, p_ref, pown_ref, l_ref, vnew_ref, g_ref, *rest, n_q):
    n_in = VALUE_PAGES_PER_STEP * HEAD_GROUPS
    pages = rest[:n_in]
    o_ref, acc_ref, vc_ref = rest[n_in:]
    c = pl.program_id(1)
    score_chunk = SCORE_PAGES_PER_STEP * PAGE_SIZE

    _gather_heads(pages, vc_ref)
    part = None
    for sub in range(VALUE_PAGES_PER_STEP // SCORE_PAGES_PER_STEP):
        term = _dot(p_ref[0, sub], vc_ref[sub * score_chunk:(sub + 1) * score_chunk, :])
        part = term if part is None else part + term

    @pl.when(c == 0)
    def _():
        acc_ref[...] = part

    @pl.when(c > 0)
    def _():
        acc_ref[...] += part

    @pl.when(c == pl.num_programs(1) - 1)
    def _():
        v_own = _pad_rows(vnew_ref[0], LANES).astype(BF16)
        full = (acc_ref[...] + _dot(pown_ref[0].astype(BF16), v_own)) / l_ref[0][:, 0:1]
        r_h = lax.broadcasted_iota(jnp.int32, full.shape, 0) // n_q
        c_h = lax.broadcasted_iota(jnp.int32, full.shape, 1) // HEAD_DIM
        o = jnp.sum(jnp.where(r_h == c_h, full, 0.0).reshape(N_HEADS, n_q, full.shape[1]), axis=0)
        gate = g_ref[0]
        o_ref[0] = o * (gate * jax.nn.sigmoid(gate))


def _moba_decode(q, g, k_new, v_new, cache_k, cache_v, page_table, n_batch, n_q):
    d = q.shape[1]
    n_pages = page_table.shape[1]
    past_len = n_pages * PAGE_SIZE
    assert n_q == SUBLANES and past_len % BLOCK == 0 and n_pages % VALUE_PAGES_PER_STEP == 0
    assert VALUE_PAGES_PER_STEP % SCORE_PAGES_PER_STEP == 0
    n_chunks = n_pages // SCORE_PAGES_PER_STEP
    chunk = SCORE_PAGES_PER_STEP * PAGE_SIZE
    chunks_per_value_step = VALUE_PAGES_PER_STEP // SCORE_PAGES_PER_STEP
    n_blocks = past_len // BLOCK
    rows = N_HEADS * n_q
    pt = page_table.reshape(-1).astype(jnp.int32)
    ck = cache_k.reshape(cache_k.shape[0] * PAGE_SIZE, N_HEADS, HEAD_DIM)
    cv = cache_v.reshape(cache_v.shape[0] * PAGE_SIZE, N_HEADS, HEAD_DIM)
    q3, g3 = q.reshape(n_batch, n_q, d), g.reshape(n_batch, n_q, d)
    kn3, vn3 = k_new.reshape(n_batch, n_q, d), v_new.reshape(n_batch, n_q, d)
    slope_cols = jnp.broadcast_to(jnp.repeat(_alibi_slopes(), n_q)[None, :], (SUBLANES, rows))

    def page_specs(pages_per_step):
        def one(i, hg):
            return pl.BlockSpec(
                (PAGE_SIZE, SUBLANES, HEAD_DIM),
                lambda b, c, pt_ref: (pt_ref[b * n_pages + c * pages_per_step + i], hg, 0))
        return [one(i, hg) for i in range(pages_per_step) for hg in range(HEAD_GROUPS)]

    tok_spec = pl.BlockSpec((1, n_q, d), lambda b, c, pt_ref: (b, 0, 0))
    row_spec = pl.BlockSpec((1, rows, LANES), lambda b, c, pt_ref: (b, 0, 0))
    k_specs = page_specs(SCORE_PAGES_PER_STEP)
    probs, p_own, denom = pl.pallas_call(
        functools.partial(_moba_decode_scores_kernel, n_q=n_q, past_len=past_len),
        out_shape=(jax.ShapeDtypeStruct((n_batch, n_chunks, rows, chunk), BF16),
                   jax.ShapeDtypeStruct((n_batch, rows, LANES), F32),
                   jax.ShapeDtypeStruct((n_batch, rows, LANES), F32)),
        grid_spec=pltpu.PrefetchScalarGridSpec(
            num_scalar_prefetch=1,
            grid=(n_batch, n_chunks),
            in_specs=[tok_spec, tok_spec,
                      pl.BlockSpec((SUBLANES, rows), lambda b, c, pt_ref: (0, 0))] + k_specs,
            out_specs=(pl.BlockSpec((1, n_chunks, rows, chunk), lambda b, c, pt_ref: (b, 0, 0, 0)),
                       row_spec, row_spec),
            scratch_shapes=[pltpu.VMEM((n_chunks, chunk, rows), F32),
                            pltpu.VMEM((n_blocks * N_HEADS, HEAD_DIM), F32),
                            pltpu.VMEM((chunk, d), BF16),
                            pltpu.VMEM((d, rows), BF16),
                            pltpu.VMEM((n_blocks, rows), F32)]),
        compiler_params=_params("arbitrary", "arbitrary"),
        name="moba_decode_scores",
    )(pt, q3, kn3, slope_cols, *([ck] * len(k_specs)))

    v_specs = page_specs(VALUE_PAGES_PER_STEP)
    out = pl.pallas_call(
        functools.partial(_moba_decode_values_kernel, n_q=n_q),
        out_shape=jax.ShapeDtypeStruct((n_batch, n_q, d), F32),
        grid_spec=pltpu.PrefetchScalarGridSpec(
            num_scalar_prefetch=1,
            grid=(n_batch, n_pages // VALUE_PAGES_PER_STEP),
            in_specs=[pl.BlockSpec((1, chunks_per_value_step, rows, chunk),
                                   lambda b, c, pt_ref: (b, c, 0, 0)),
                      row_spec, row_spec, tok_spec, tok_spec] + v_specs,
            out_specs=tok_spec,
            scratch_shapes=[pltpu.VMEM((rows, d), F32),
                            pltpu.VMEM((VALUE_PAGES_PER_STEP * PAGE_SIZE, d), BF16)]),
        compiler_params=_params("arbitrary", "arbitrary"),
        name="moba_decode_values",
    )(pt, probs, p_own, denom, vn3, g3, *([cv] * len(v_specs)))
    return out.reshape(n_batch * n_q, d).astype(BF16)


def _cast_kernel(src_ref, dst_ref):
    dst_ref[...] = src_ref[...].astype(BF16)


def _cast_layer(w, layer, slab=128):
    _, r, c = w.shape
    return pl.pallas_call(
        _cast_kernel,
        out_shape=jax.ShapeDtypeStruct((1, r, c), BF16),
        grid=(r // slab,),
        in_specs=[pl.BlockSpec((None, slab, c), lambda i: (layer, i, 0))],
        out_specs=pl.BlockSpec((None, slab, c), lambda i: (0, i, 0)),
        compiler_params=_params("arbitrary"),
        name="cast_layer",
    )(w)


def _trunks(x_long, x_short, conv_prev_long, conv_prev_short, params, attend_long, attend_short):
    conv_w, ln_g, ln_b = params["conv_w"], params["ln_g"], params["ln_b"]
    n_l, t_l, d = x_long.shape
    n_s, t_s, _ = x_short.shape
    xl32, xs32 = x_long.reshape(n_l * t_l, d), x_short.reshape(n_s * t_s, d)
    xl16, xs16 = xl32.astype(BF16), xs32.astype(BF16)
    w16 = {("w_in_a", 0): _cast_layer(params["w_in_a"], 0)}

    def jobs(*names):
        return [(params[n] if params[n].ndim == 3 else params[n][None], l) for n, l in names]

    def keep(names, copies):
        w16.update(zip(names, copies))

    conv_l, conv_s = [], []
    ahead = [
        [("w_out_a", 0), ("w_in_a", 1)],
        [("w_out_a", 1), ("w_kv", 0), ("w_in_b", 0)],
    ]
    for l in range(N_A):
        yl16, rows_l, copies, (ys16, rows_s) = _conv_proj(
            xl16, w16["w_in_a", l], 0, conv_w[l], conv_prev_long[l], t_l, jobs(*ahead[l]),
            short=(xs16, conv_prev_short[l], t_s))
        keep(ahead[l], copies)
        conv_l.append(rows_l)
        conv_s.append(rows_s)
        xl32, xl16, xs32, xs16 = _out_proj_ln(yl16, w16["w_out_a", l], 0, xl32, ln_g[l], ln_b[l],
                                              short=(ys16, xs32))
    names = [("w_in_b", 1)]
    kl, vl, copies, (ks, vs) = _dual_proj(xl16, w16["w_kv", 0], 0, jobs(*names), short=xs16)
    keep(names, copies)
    for l in range(N_A, DEPTH):
        names = [("w_out_b", 0), ("w_out_b", 1)] if l == N_A else []
        ql, gl, copies, (qs, gs) = _dual_proj(xl16, w16["w_in_b", l - N_A], 0, jobs(*names), short=xs16)
        keep(names, copies)
        ol16 = attend_long(ql, gl, kl, vl)
        os16 = attend_short(qs, gs, ks, vs)
        xl32, xl16, xs32, xs16 = _out_proj_ln(ol16, w16["w_out_b", l - N_A], 0, xl32, ln_g[l], ln_b[l],
                                              short=(os16, xs32))
    heads_l, heads_s = (n_l, t_l, N_HEADS, HEAD_DIM), (n_s, t_s, N_HEADS, HEAD_DIM)
    return ((xl32.reshape(n_l, t_l, d), kl.reshape(heads_l), vl.reshape(heads_l), jnp.stack(conv_l)),
            (xs32.reshape(n_s, t_s, d), ks.reshape(heads_s), vs.reshape(heads_s), jnp.stack(conv_s)))


def kernel(x_prompt, x_sample, cache_k, cache_v, state_conv, page_table, w_in_a, conv_w, w_out_a,
           w_kv, w_in_b, w_out_b, ln_g, ln_b):
    params = dict(w_in_a=w_in_a, conv_w=conv_w, w_out_a=w_out_a, w_kv=w_kv, w_in_b=w_in_b,
                  w_out_b=w_out_b, ln_g=ln_g, ln_b=ln_b)
    n_prompt, seq, d = x_prompt.shape
    n_dec, dec_seq, _ = x_sample.shape
    zero_conv = jnp.zeros((N_A, n_prompt, CONV_W - 1, d), x_prompt.dtype)
    prompt, sample = _trunks(
        x_prompt, x_sample, zero_conv, state_conv, params,
        lambda q, g, k, v: _moba_prefill(q, k, v, g, n_prompt, seq),
        lambda q, g, k, v: _moba_decode(q, g, k, v, cache_k, cache_v, page_table, n_dec, dec_seq))
    y_prompt, k_prompt, v_prompt, conv_prompt = prompt
    y_sample, k_sample, v_sample, conv_sample = sample
    return (y_prompt, y_sample, k_prompt, v_prompt, conv_prompt, k_sample, v_sample, conv_sample)
```

```python
import functools
import math

import numpy as np
import jax
import jax.numpy as jnp
from jax import lax
from jax.experimental import pallas as pl
from jax.experimental.pallas import tpu as pltpu

N_HEADS = 32
HEAD_DIM = 128
CONV_W = 3
BLOCK = 256
TOP_K = 3
PAGE_SIZE = 128
DEPTH = 4
N_A = DEPTH // 2
ALPHA = (2.0 * DEPTH) ** 0.25
LN_EPS = 1e-5
SCALE = 1.0 / math.sqrt(HEAD_DIM)
LOG2_E = math.log2(math.e)

SUBLANES = 8
LANES = 128
VMEM_LIMIT_BYTES = 56 * 1024 * 1024
LN_VMEM_LIMIT_BYTES = 58 * 1024 * 1024
MASK_BIG = 2.0 ** 100
SCORE_PAGES_PER_STEP = 4
VALUE_PAGES_PER_STEP = 8
HEAD_GROUPS = N_HEADS // SUBLANES
ALIBI_PARTS = 4
HEADS_PER_STEP = 2

F32 = jnp.float32
BF16 = jnp.bfloat16


def _params(*semantics, vmem_limit=VMEM_LIMIT_BYTES):
    return pltpu.CompilerParams(dimension_semantics=semantics, vmem_limit_bytes=vmem_limit)


def _dot(a, b):
    return jnp.dot(a, b, preferred_element_type=F32)


def _dot_nt(a, b):
    return lax.dot_general(a, b, (((1,), (1,)), ((), ())), preferred_element_type=F32)


def _cast_jobs(jobs, n_steps, step_of):
    in_specs, out_specs, out_shapes, operands = [], [], [], []
    for w, layer in jobs:
        _, r, c = w.shape
        assert r % n_steps == 0 and (r // n_steps) % 16 == 0
        slab = r // n_steps
        in_specs.append(pl.BlockSpec(
            (None, slab, c), functools.partial(lambda *ids, layer: (layer, step_of(*ids), 0), layer=layer)))
        out_specs.append(pl.BlockSpec((None, slab, c), lambda *ids: (0, step_of(*ids), 0)))
        out_shapes.append(jax.ShapeDtypeStruct((1, r, c), BF16))
        operands.append(w)
    return in_specs, out_specs, out_shapes, operands


def _run_casts(src_refs, dst_refs):
    for src, dst in zip(src_refs, dst_refs):
        dst[...] = src[...].astype(BF16)


def _gated_conv(b_g, u, u_m1, u_m2, z, cw_ref):
    conv = cw_ref[0:1, :] * u_m2 + cw_ref[1:2, :] * u_m1 + cw_ref[2:3, :] * u
    return (b_g * conv) * (z * jax.nn.sigmoid(z))


def _conv_rows(x, w_refs, cw_ref, u_prev):
    wb_ref, wc_ref, wh_ref, wz_ref = w_refs
    u = _dot(x, wc_ref[...]) * _dot(x, wh_ref[...])
    u_m1, u_m2 = u_prev(u)
    return _gated_conv(_dot(x, wb_ref[...]), u, u_m1, u_m2, _dot(x, wz_ref[...]), cw_ref), u


def _conv_proj_kernel(x_ref, wb_ref, wc_ref, wh_ref, wz_ref, cw_ref, prev_ref, *rest,
                      tiles_per_seq, n_casts, short_len):
    n_short = 3 if short_len else 0
    short_in = rest[:n_short]
    cast_src = rest[n_short:n_short + n_casts]
    outs = rest[n_short + n_casts:]
    y_ref, tail_ref = outs[:2]
    short_out = outs[2:2 + (2 if short_len else 0)]
    cast_dst = outs[2 + len(short_out):2 + len(short_out) + n_casts]
    carry_ref = outs[-1]
    w_refs = (wb_ref, wc_ref, wh_ref, wz_ref)
    _run_casts(cast_src, cast_dst)
    m = pl.program_id(1)

    @pl.when(lax.rem(m, tiles_per_seq) == 0)
    def _():
        carry_ref[...] = prev_ref[0]

    def carried(u):
        row = lax.broadcasted_iota(jnp.int32, u.shape, 0)
        c_m2 = carry_ref[SUBLANES - 2:SUBLANES - 1, :]
        c_m1 = carry_ref[SUBLANES - 1:SUBLANES, :]
        u_m1 = jnp.where(row == 0, c_m1, pltpu.roll(u, 1, 0))
        u_m2 = jnp.where(row == 0, c_m2, jnp.where(row == 1, c_m1, pltpu.roll(u, 2, 0)))
        return u_m1, u_m2

    y, u = _conv_rows(x_ref[...], w_refs, cw_ref, carried)
    y_ref[...] = y.astype(y_ref.dtype)
    tail = u[u.shape[0] - SUBLANES:, :]
    carry_ref[...] = tail
    tail_ref[0] = tail

    if short_len:
        xs_ref, pm1_ref, pm2_ref = short_in
        ys_ref, us_ref = short_out

        @pl.when(m == 0)
        def _():
            def preceded(u_s):
                pos = lax.rem(lax.broadcasted_iota(jnp.int32, u_s.shape, 0), short_len)
                return (jnp.where(pos == 0, pm1_ref[...], pltpu.roll(u_s, 1, 0)),
                        jnp.where(pos < 2, pm2_ref[...], pltpu.roll(u_s, 2, 0)))

            y_s, u_s = _conv_rows(xs_ref[...], w_refs, cw_ref, preceded)
            ys_ref[...] = y_s.astype(ys_ref.dtype)
            us_ref[...] = u_s


def _conv_proj(x16, w_in16, layer, conv_w, conv_prev, seq_len, casts=(), short=None):
    m_rows, d = x16.shape
    n_seq = m_rows // seq_len
    tn = 256
    nj = d // tn
    w_specs = [pl.BlockSpec((None, d, tn),
                            functools.partial(lambda j, m, g: (layer, 0, g * nj + j), g=g))
               for g in range(4)]
    tm = 1024 if seq_len % 1024 == 0 else 512
    assert seq_len % tm == 0
    tiles_per_seq = seq_len // tm
    nm = m_rows // tm
    prev8 = jnp.pad(conv_prev.astype(F32), ((0, 0), (SUBLANES - (CONV_W - 1), 0), (0, 0)))
    c_in, c_out, c_shapes, c_ops = _cast_jobs(casts, nj * nm, lambda j, m: j * nm + m)
    s_in, s_out, s_shapes, s_ops, short_len = [], [], [], [], 0
    if short is not None:
        xs16, prev_s, short_len = short
        ms = xs16.shape[0]
        n_s = ms // short_len
        assert short_len >= CONV_W - 1 and ms % 16 == 0
        prev_s = prev_s.astype(F32)
        zeros = jnp.zeros((n_s, short_len - 1, d), F32)
        pm1 = jnp.concatenate([prev_s[:, 1:2], zeros], axis=1).reshape(ms, d)
        pm2 = jnp.concatenate([prev_s, zeros[:, 1:]], axis=1).reshape(ms, d)
        col = pl.BlockSpec((ms, tn), lambda j, m: (0, j))
        s_in = [pl.BlockSpec((ms, d), lambda j, m: (0, 0)), col, col]
        s_out = [col, col]
        s_shapes = [jax.ShapeDtypeStruct((ms, d), BF16), jax.ShapeDtypeStruct((ms, d), F32)]
        s_ops = [xs16, pm1, pm2]
    outs = pl.pallas_call(
        functools.partial(_conv_proj_kernel, tiles_per_seq=tiles_per_seq, n_casts=len(casts),
                          short_len=short_len),
        out_shape=(jax.ShapeDtypeStruct((m_rows, d), BF16),
                   jax.ShapeDtypeStruct((n_seq, SUBLANES, d), F32), *s_shapes, *c_shapes),
        grid=(nj, nm),
        in_specs=[pl.BlockSpec((tm, d), lambda j, m: (m, 0))] + w_specs + [
            pl.BlockSpec((CONV_W, tn), lambda j, m: (0, j)),
            pl.BlockSpec((1, SUBLANES, tn), lambda j, m: (m // tiles_per_seq, 0, j))] + s_in + c_in,
        out_specs=(pl.BlockSpec((tm, tn), lambda j, m: (m, j)),
                   pl.BlockSpec((1, SUBLANES, tn), lambda j, m: (m // tiles_per_seq, 0, j)),
                   *s_out, *c_out),
        scratch_shapes=[pltpu.VMEM((SUBLANES, tn), F32)],
        compiler_params=_params("arbitrary", "arbitrary"),
        name="conv_proj",
    )(x16, w_in16, w_in16, w_in16, w_in16, conv_w, prev8, *s_ops, *c_ops)
    y, tail = outs[:2]
    short_res = None
    if short is not None:
        y_s, u_s = outs[2:4]
        short_res = (y_s, u_s.reshape(n_s, short_len, d)[:, short_len - (CONV_W - 1):, :])
    return y, tail[:, SUBLANES - (CONV_W - 1):, :], list(outs[2 + len(s_shapes):]), short_res


def _fold_lanes(v):
    out = v[:, :LANES]
    for t in range(1, v.shape[1] // LANES):
        out = out + v[:, t * LANES:(t + 1) * LANES]
    return out


def _ln_collect(r, j, nj, d, r_ref, shift_ref, sum_ref, sq_ref, mean_ref, rstd_ref):
    r_ref[j] = r

    @pl.when(j == 0)
    def _():
        shift_ref[...] = jnp.broadcast_to(jnp.mean(r, axis=-1, keepdims=True), shift_ref.shape)
        sum_ref[...] = jnp.zeros_like(sum_ref)
        sq_ref[...] = jnp.zeros_like(sq_ref)

    dlt = r - shift_ref[:, 0:1]
    sum_ref[...] += _fold_lanes(dlt)
    sq_ref[...] += _fold_lanes(dlt * dlt)

    @pl.when(j == nj - 1)
    def _():
        e1 = jnp.sum(sum_ref[...], axis=-1, keepdims=True) / d
        e2 = jnp.sum(sq_ref[...], axis=-1, keepdims=True) / d
        var = jnp.maximum(e2 - e1 * e1, 0.0)
        mean_ref[...] = jnp.broadcast_to(shift_ref[:, 0:1] + e1, mean_ref.shape)
        rstd_ref[...] = jnp.broadcast_to(lax.rsqrt(var + LN_EPS), rstd_ref.shape)


def _ln_emit(j, r_ref, mean_ref, rstd_ref, g_ref, b_ref, xo_ref, xo16_ref):
    o = (r_ref[j] - mean_ref[:, 0:1]) * rstd_ref[:, 0:1] * g_ref[...] + b_ref[...]
    xo_ref[...] = o
    xo16_ref[...] = o.astype(xo16_ref.dtype)


def _out_proj_ln_kernel(y_ref, w_ref, xres_ref, g_ref, b_ref, *rest, nm, nj, d, has_short):
    n_s = 2 if has_short else 0
    short_in = rest[:n_s]
    xo_ref, xo16_ref = rest[n_s:n_s + 2]
    short_out = rest[n_s + 2:2 * n_s + 2]
    scratch = rest[2 * n_s + 2:]
    main_state, short_state = scratch[:6], scratch[6:]
    mm = pl.program_id(0)
    j = pl.program_id(1)

    @pl.when(mm > 0)
    def _():
        _ln_emit(j, main_state[0], main_state[4], main_state[5], g_ref, b_ref, xo_ref, xo16_ref)

    @pl.when(mm < nm)
    def _():
        r = ALPHA * xres_ref[...] + _dot(y_ref[...], w_ref[...])
        _ln_collect(r, j, nj, d, *main_state)

    if has_short:
        ys_ref, xres_s_ref = short_in

        @pl.when(mm == 1)
        def _():
            _ln_emit(j, short_state[0], short_state[4], short_state[5], g_ref, b_ref, *short_out)

        @pl.when(mm == 0)
        def _():
            r_s = ALPHA * xres_s_ref[...] + _dot(ys_ref[...], w_ref[...])
            _ln_collect(r_s, j, nj, d, *short_state)


def _out_proj_ln(y16, w16, layer, x_res, gain, bias, short=None):
    m_rows, k = y16.shape
    d = w16.shape[2]
    tm = min(1024, m_rows)
    tn = 512
    assert m_rows % tm == 0 and d % tn == 0
    nj = d // tn
    nm = m_rows // tm
    row_in = lambda mm: jnp.minimum(mm, nm - 1)
    col_in = lambda mm, j: jnp.where(mm < nm, j, nj - 1)
    out_idx = lambda mm, j: (jnp.maximum(mm - 1, 0), jnp.where(mm == 0, 0, j))

    def state(rows):
        stat = pltpu.VMEM((rows, LANES), F32)
        return [pltpu.VMEM((nj, rows, tn), F32), stat, stat, stat, stat, stat]

    s_in, s_out, s_shapes, s_ops, s_state = [], [], [], [], []
    if short is not None:
        ys16, x_res_s = short
        ms = ys16.shape[0]
        s_in = [pl.BlockSpec((ms, k), lambda mm, j: (0, 0)),
                pl.BlockSpec((ms, tn), lambda mm, j: (0, jnp.where(mm == 0, j, nj - 1)))]
        s_idx = lambda mm, j: (0, jnp.where(mm == 0, 0, jnp.where(mm == 1, j, nj - 1)))
        s_out = [pl.BlockSpec((ms, tn), s_idx), pl.BlockSpec((ms, tn), s_idx)]
        s_shapes = [jax.ShapeDtypeStruct((ms, d), F32), jax.ShapeDtypeStruct((ms, d), BF16)]
        s_ops = [ys16, x_res_s]
        s_state = state(ms)
        assert nm >= 1
    return pl.pallas_call(
        functools.partial(_out_proj_ln_kernel, nm=nm, nj=nj, d=d, has_short=short is not None),
        out_shape=(jax.ShapeDtypeStruct((m_rows, d), F32), jax.ShapeDtypeStruct((m_rows, d), BF16),
                   *s_shapes),
        grid=(nm + 1, nj),
        in_specs=[pl.BlockSpec((tm, k), lambda mm, j: (row_in(mm), 0)),
                  pl.BlockSpec((None, k, tn), lambda mm, j: (layer, 0, col_in(mm, j))),
                  pl.BlockSpec((tm, tn), lambda mm, j: (row_in(mm), col_in(mm, j))),
                  pl.BlockSpec((1, tn), lambda mm, j: (0, j)),
                  pl.BlockSpec((1, tn), lambda mm, j: (0, j))] + s_in,
        out_specs=(pl.BlockSpec((tm, tn), out_idx), pl.BlockSpec((tm, tn), out_idx), *s_out),
        scratch_shapes=state(tm) + s_state,
        compiler_params=_params("arbitrary", "arbitrary", vmem_limit=LN_VMEM_LIMIT_BYTES),
        name="out_proj_ln",
    )(y16, w16, x_res, gain.reshape(1, d), bias.reshape(1, d), *s_ops)


def _dual_proj_kernel(x_ref, wa_ref, wb_ref, *rest, n_casts, has_short):
    n_s = 1 if has_short else 0
    cast_src = rest[n_s:n_s + n_casts]
    outs = rest[n_s + n_casts:]
    oa_ref, ob_ref = outs[:2]
    _run_casts(cast_src, outs[2 + 2 * n_s:])
    x = x_ref[...]
    oa_ref[...] = _dot(x, wa_ref[...])
    ob_ref[...] = _dot(x, wb_ref[...])
    if has_short:
        xs_ref = rest[0]
        oas_ref, obs_ref = outs[2:4]

        @pl.when(pl.program_id(0) == 0)
        def _():
            xs = xs_ref[...]
            oas_ref[...] = _dot(xs, wa_ref[...])
            obs_ref[...] = _dot(xs, wb_ref[...])


def _dual_proj(x16, w16, layer, casts=(), short=None):
    m_rows, k = x16.shape
    n_half = w16.shape[2] // 2
    tm = next(t for t in (1024, 512, m_rows) if m_rows % t == 0)
    tn = 512
    nj = n_half // tn
    out = jax.ShapeDtypeStruct((m_rows, n_half), F32)
    nm = m_rows // tm
    c_in, c_out, c_shapes, c_ops = _cast_jobs(casts, nm * nj, lambda m, j: m * nj + j)
    s_in, s_out, s_shapes, s_ops = [], [], [], []
    if short is not None:
        ms = short.shape[0]
        s_idx = lambda m, j: (0, jnp.where(m == 0, j, nj - 1))
        s_in = [pl.BlockSpec((ms, k), lambda m, j: (0, 0))]
        s_out = [pl.BlockSpec((ms, tn), s_idx), pl.BlockSpec((ms, tn), s_idx)]
        s_shapes = [jax.ShapeDtypeStruct((ms, n_half), F32)] * 2
        s_ops = [short]
    outs = pl.pallas_call(
        functools.partial(_dual_proj_kernel, n_casts=len(casts), has_short=short is not None),
        out_shape=(out, out, *s_shapes, *c_shapes),
        grid=(nm, nj),
        in_specs=[pl.BlockSpec((tm, k), lambda m, j: (m, 0)),
                  pl.BlockSpec((None, k, tn), lambda m, j: (layer, 0, j)),
                  pl.BlockSpec((None, k, tn), lambda m, j: (layer, 0, nj + j))] + s_in + c_in,
        out_specs=(pl.BlockSpec((tm, tn), lambda m, j: (m, j)),
                   pl.BlockSpec((tm, tn), lambda m, j: (m, j)), *s_out, *c_out),
        compiler_params=_params("arbitrary", "arbitrary"),
        name="dual_proj",
    )(x16, w16, w16, *s_ops, *c_ops)
    n_short = len(s_shapes)
    return outs[0], outs[1], list(outs[2 + n_short:]), tuple(outs[2:2 + n_short])


def _moba_prefill_kernel(q_ref, k_ref, v_ref, g_ref, kx_ref, qx_ref, o_ref, *scratch):
    seq = q_ref.shape[0]
    nb = seq // BLOCK
    per_head = 3 + nb
    r_i = lax.broadcasted_iota(jnp.int32, (BLOCK, BLOCK), 0)
    c_i = lax.broadcasted_iota(jnp.int32, (BLOCK, BLOCK), 1)
    causal = jnp.where(c_i <= r_i, 0.0, -jnp.inf)
    state = []
    for hh in range(HEADS_PER_STEP):
        qa_ref, ka_ref, vb_ref = scratch[hh * per_head:hh * per_head + 3]
        s_refs = scratch[hh * per_head + 3:(hh + 1) * per_head]
        cols = slice(hh * HEAD_DIM, (hh + 1) * HEAD_DIM)
        q16 = q_ref[:, cols].astype(BF16)
        k = k_ref[:, cols]
        kmean = jnp.mean(k.reshape(nb, BLOCK, HEAD_DIM), axis=1)
        gate_t = _dot_nt(kmean.astype(BF16), q16)
        blk = lax.broadcasted_iota(jnp.int32, (nb, seq), 0)
        q_blk = lax.broadcasted_iota(jnp.int32, (nb, seq), 1) // BLOCK
        rank = jnp.zeros((nb, seq), jnp.int32)
        for other in range(nb):
            g_o = gate_t[other:other + 1, :]
            beats = (g_o > gate_t) | ((g_o == gate_t) & (other < blk))
            rank += jnp.where((q_blk > other) & beats, 1, 0)
        dropped = ((blk < q_blk) & (rank >= TOP_K)).astype(F32)
        lane = lax.broadcasted_iota(jnp.int32, (nb, LANES), 1)
        spread = jnp.where(lane == lax.broadcasted_iota(jnp.int32, (nb, LANES), 0), -MASK_BIG, 0.0)
        sel_bias = lax.dot_general(dropped, spread, (((0,), (0,)), ((), ())),
                                   preferred_element_type=F32)
        qa_ref[:, :HEAD_DIM] = q16
        qa_ref[:, HEAD_DIM:] = (sel_bias + qx_ref[hh, 0:1, :]).astype(BF16)
        ka_ref[:, :HEAD_DIM] = k.astype(BF16)
        ka_ref[:, HEAD_DIM:] = kx_ref[...]
        vb_ref[:, :HEAD_DIM] = v_ref[:, cols].astype(BF16)
        ones_col = lax.broadcasted_iota(jnp.int32, (seq, HEAD_DIM), 1) == 0
        vb_ref[:, HEAD_DIM:] = jnp.where(ones_col, 1.0, 0.0).astype(BF16)
        state.append((qa_ref, ka_ref, vb_ref, s_refs, cols))
    for i in range(nb):
        for qa_ref, ka_ref, vb_ref, s_refs, cols in state:
            s_refs[i][...] = _dot_nt(qa_ref[i * BLOCK:, :], ka_ref[i * BLOCK:(i + 1) * BLOCK, :])
        for qa_ref, ka_ref, vb_ref, s_refs, cols in state:
            rows = slice(i * BLOCK, (i + 1) * BLOCK)
            n_keys = (i + 1) * BLOCK
            parts = [s_refs[n][(i - n) * BLOCK:(i - n + 1) * BLOCK, :] for n in range(i)]
            parts.append(s_refs[i][0:BLOCK, :] + causal)
            s = parts[0] if i == 0 else jnp.concatenate(parts, axis=1)
            m = jnp.max(s, axis=-1, keepdims=True)
            p = jnp.exp2((s - m) * (SCALE * LOG2_E))
            ol = _dot(p.astype(BF16), vb_ref[0:n_keys, :])
            o = ol[:, :HEAD_DIM] / ol[:, HEAD_DIM:HEAD_DIM + 1]
            gate = g_ref[rows, cols]
            o_ref[rows, cols] = (o * (gate * jax.nn.sigmoid(gate))).astype(o_ref.dtype)


def _alibi_slopes():
    return 2.0 ** (-8.0 * jnp.arange(1, N_HEADS + 1, dtype=F32) / N_HEADS)


def _moba_prefill(q, k, v, g, n_batch, seq):
    nb = seq // BLOCK
    assert nb <= SUBLANES and seq % BLOCK == 0
    pos = np.arange(seq)
    kx = np.zeros((seq, LANES), np.float32)
    kx[pos, pos // BLOCK] = 1.0
    kx[:, SUBLANES:SUBLANES + ALIBI_PARTS] = (pos % BLOCK)[:, None]
    kx[:, SUBLANES + ALIBI_PARTS:SUBLANES + 2 * ALIBI_PARTS] = ((pos // BLOCK) * BLOCK)[:, None]
    rest = _alibi_slopes() / SCALE
    pieces = []
    for _ in range(ALIBI_PARTS):
        piece = rest.astype(BF16).astype(F32)
        pieces.append(piece)
        rest = rest - piece
    pieces = jnp.stack(pieces, axis=1)
    qx = jnp.zeros((N_HEADS, LANES), F32)
    qx = qx.at[:, SUBLANES:SUBLANES + ALIBI_PARTS].set(pieces)
    qx = qx.at[:, SUBLANES + ALIBI_PARTS:SUBLANES + 2 * ALIBI_PARTS].set(pieces)
    qx = jnp.broadcast_to(qx[:, None, :], (N_HEADS, SUBLANES, LANES))
    pair_spec = pl.BlockSpec((seq, HEADS_PER_STEP * HEAD_DIM), lambda b, h: (b, h))
    wide = pltpu.VMEM((seq, 2 * HEAD_DIM), BF16)
    per_head = [wide, wide, wide] + [pltpu.VMEM(((nb - i) * BLOCK, BLOCK), F32) for i in range(nb)]
    return pl.pallas_call(
        _moba_prefill_kernel,
        out_shape=jax.ShapeDtypeStruct(q.shape, BF16),
        grid=(n_batch, N_HEADS // HEADS_PER_STEP),
        in_specs=[pair_spec, pair_spec, pair_spec, pair_spec,
                  pl.BlockSpec((seq, LANES), lambda b, h: (0, 0)),
                  pl.BlockSpec((HEADS_PER_STEP, SUBLANES, LANES), lambda b, h: (h, 0, 0))],
        out_specs=pair_spec,
        scratch_shapes=per_head * HEADS_PER_STEP,
        compiler_params=_params("arbitrary", "arbitrary"),
        name="moba_prefill",
    )(q, k, v, g, jnp.asarray(kx, BF16), qx)


def _block_diag_rows(x, n_q):
    rows = N_HEADS * n_q
    tiled = jnp.concatenate([x] * N_HEADS, axis=0)
    r_h = lax.broadcasted_iota(jnp.int32, (rows, x.shape[1]), 0) // n_q
    c_h = lax.broadcasted_iota(jnp.int32, (rows, x.shape[1]), 1) // HEAD_DIM
    return jnp.where(r_h == c_h, tiled, 0.0)


def _pad_rows(x, rows):
    return jnp.concatenate([x, jnp.zeros((rows - x.shape[0], x.shape[1]), x.dtype)], axis=0)


def _head_major(page_refs):
    n_pages = len(page_refs) // HEAD_GROUPS
    cols = []
    for h in range(N_HEADS):
        hg, sl = divmod(h, SUBLANES)
        rows = [page_refs[i * HEAD_GROUPS + hg].reshape(PAGE_SIZE * SUBLANES, HEAD_DIM)[
            pl.ds(sl, PAGE_SIZE, stride=SUBLANES), :] for i in range(n_pages)]
        cols.append(jnp.concatenate(rows, axis=0).astype(BF16))
    return jnp.concatenate(cols, axis=1)


def _moba_decode_scores_kernel(pt_ref, q_ref, knew_ref, slope_ref, *rest, n_q, past_len):
    n_in = SCORE_PAGES_PER_STEP * HEAD_GROUPS
    pages = rest[:n_in]
    p_ref, pown_ref, l_ref, s_ref, ksum_ref, qbd_ref, pick_ref = rest[n_in:]
    c = pl.program_id(1)
    n_chunks = pl.num_programs(1)
    chunk = SCORE_PAGES_PER_STEP * PAGE_SIZE
    blocks_per_chunk = chunk // BLOCK
    n_blocks = past_len // BLOCK
    rows = N_HEADS * n_q

    @pl.when(c == 0)
    def _():
        qbd_ref[...] = _block_diag_rows(q_ref[0], n_q).T.astype(BF16)

    pages_per_block = BLOCK // PAGE_SIZE
    for bi in range(blocks_per_chunk):
        for hg in range(HEAD_GROUPS):
            tot = None
            for i in range(bi * pages_per_block, (bi + 1) * pages_per_block):
                part = jnp.sum(pages[i * HEAD_GROUPS + hg][...], axis=0)
                tot = part if tot is None else tot + part
            row = ((c * blocks_per_chunk + bi) * HEAD_GROUPS + hg) * SUBLANES
            ksum_ref[pl.ds(pl.multiple_of(row, SUBLANES), SUBLANES), :] = tot
    s_ref[c] = _dot(_head_major(pages), qbd_ref[...])

    @pl.when(c == n_chunks - 1)
    def _():
        qbd = qbd_ref[...]
        ksum = jnp.concatenate([ksum_ref[pl.ds(h, n_blocks, stride=N_HEADS), :]
                                for h in range(N_HEADS)], axis=1)
        kmean = (ksum / BLOCK).astype(BF16)
        gate = _dot(kmean, qbd)
        idx = lax.broadcasted_iota(jnp.int32, gate.shape, 0)
        picked = jnp.zeros(gate.shape, jnp.bool_)
        for _ in range(TOP_K):
            best = jnp.max(gate, axis=0, keepdims=True)
            first = jnp.min(jnp.where(gate == best, idx, n_blocks), axis=0, keepdims=True)
            hit = idx == first
            picked = picked | hit
            gate = jnp.where(hit, -jnp.inf, gate)
        pick_ref[...] = picked.astype(F32)

        slope = slope_ref[0:1, :]
        q_i = lax.rem(lax.broadcasted_iota(jnp.int32, (1, rows), 1), n_q)
        t = (past_len + q_i).astype(F32)

        def mask_chunk(ci, m_run):
            pos = (ci * chunk + lax.broadcasted_iota(jnp.int32, (chunk, 1), 0)).astype(F32)
            s = s_ref[ci] * SCALE - slope * (t - pos)
            parts = []
            for bi in range(blocks_per_chunk):
                keep = pick_ref[pl.ds(ci * blocks_per_chunk + bi, 1), :]
                parts.append(jnp.where(keep > 0.5, s[bi * BLOCK:(bi + 1) * BLOCK], -jnp.inf))
            s = jnp.concatenate(parts, axis=0)
            s_ref[ci] = s
            return jnp.maximum(m_run, jnp.max(s, axis=0, keepdims=True))

        m_past = lax.fori_loop(0, n_chunks, mask_chunk, jnp.full((1, rows), -jnp.inf, F32))

        k_own = _pad_rows(knew_ref[0], LANES).astype(BF16)
        j_own = lax.broadcasted_iota(jnp.int32, (LANES, rows), 0)
        s_own = _dot(k_own, qbd) * SCALE - slope * (q_i - j_own).astype(F32)
        s_own = jnp.where(j_own <= q_i, s_own, -jnp.inf)
        m_all = jnp.maximum(m_past, jnp.max(s_own, axis=0, keepdims=True))
        p_own = jnp.exp(s_own - m_all)
        pown_ref[0] = p_own.T

        def exp_chunk(ci, l_run):
            p = jnp.exp(s_ref[ci] - m_all)
            p_ref[0, ci] = p.T.astype(p_ref.dtype)
            return l_run + jnp.sum(p, axis=0, keepdims=True)

        l_all = lax.fori_loop(0, n_chunks, exp_chunk, jnp.sum(p_own, axis=0, keepdims=True))
        l_ref[0] = jnp.broadcast_to(l_all, (LANES, rows)).T


def _moba_decode_values_kernel(pt_ref, p_ref, pown_ref, l_ref, vnew_ref, g_ref, *rest, n_q):
    n_in = VALUE_PAGES_PER_STEP * HEAD_GROUPS
    pages = rest[:n_in]
    o_ref, acc_ref = rest[n_in:]
    c = pl.program_id(1)
    pages_per_block = BLOCK // PAGE_SIZE
    blocks_per_chunk = SCORE_PAGES_PER_STEP // pages_per_block

    part = None
    for blk in range(VALUE_PAGES_PER_STEP // pages_per_block):
        slabs = pages[blk * pages_per_block * HEAD_GROUPS:(blk + 1) * pages_per_block * HEAD_GROUPS]
        sub, within = divmod(blk, blocks_per_chunk)
        term = _dot(p_ref[0, sub, :, within * BLOCK:(within + 1) * BLOCK], _head_major(slabs))
        part = term if part is None else part + term

    @pl.when(c == 0)
    def _():
        acc_ref[...] = part

    @pl.when(c > 0)
    def _():
        acc_ref[...] += part

    @pl.when(c == pl.num_programs(1) - 1)
    def _():
        v_own = _pad_rows(vnew_ref[0], LANES).astype(BF16)
        full = (acc_ref[...] + _dot(pown_ref[0].astype(BF16), v_own)) / l_ref[0][:, 0:1]
        r_h = lax.broadcasted_iota(jnp.int32, full.shape, 0) // n_q
        c_h = lax.broadcasted_iota(jnp.int32, full.shape, 1) // HEAD_DIM
        o = jnp.sum(jnp.where(r_h == c_h, full, 0.0).reshape(N_HEADS, n_q, full.shape[1]), axis=0)
        gate = g_ref[0]
        o_ref[0] = o * (gate * jax.nn.sigmoid(gate))


def _moba_decode(q, g, k_new, v_new, cache_k, cache_v, page_table, n_batch, n_q):
    d = q.shape[1]
    n_pages = page_table.shape[1]
    past_len = n_pages * PAGE_SIZE
    assert n_q == SUBLANES and past_len % BLOCK == 0 and n_pages % VALUE_PAGES_PER_STEP == 0
    assert VALUE_PAGES_PER_STEP % SCORE_PAGES_PER_STEP == 0
    n_chunks = n_pages // SCORE_PAGES_PER_STEP
    chunk = SCORE_PAGES_PER_STEP * PAGE_SIZE
    chunks_per_value_step = VALUE_PAGES_PER_STEP // SCORE_PAGES_PER_STEP
    n_blocks = past_len // BLOCK
    rows = N_HEADS * n_q
    pt = page_table.reshape(-1).astype(jnp.int32)
    ck = cache_k.reshape(cache_k.shape[0] * PAGE_SIZE, N_HEADS, HEAD_DIM)
    cv = cache_v.reshape(cache_v.shape[0] * PAGE_SIZE, N_HEADS, HEAD_DIM)
    q3, g3 = q.reshape(n_batch, n_q, d), g.reshape(n_batch, n_q, d)
    kn3, vn3 = k_new.reshape(n_batch, n_q, d), v_new.reshape(n_batch, n_q, d)
    slope_cols = jnp.broadcast_to(jnp.repeat(_alibi_slopes(), n_q)[None, :], (SUBLANES, rows))

    def page_specs(pages_per_step):
        def one(i, hg):
            return pl.BlockSpec(
                (PAGE_SIZE, SUBLANES, HEAD_DIM),
                lambda b, c, pt_ref: (pt_ref[b * n_pages + c * pages_per_step + i], hg, 0))
        return [one(i, hg) for i in range(pages_per_step) for hg in range(HEAD_GROUPS)]

    tok_spec = pl.BlockSpec((1, n_q, d), lambda b, c, pt_ref: (b, 0, 0))
    row_spec = pl.BlockSpec((1, rows, LANES), lambda b, c, pt_ref: (b, 0, 0))
    k_specs = page_specs(SCORE_PAGES_PER_STEP)
    probs, p_own, denom = pl.pallas_call(
        functools.partial(_moba_decode_scores_kernel, n_q=n_q, past_len=past_len),
        out_shape=(jax.ShapeDtypeStruct((n_batch, n_chunks, rows, chunk), BF16),
                   jax.ShapeDtypeStruct((n_batch, rows, LANES), F32),
                   jax.ShapeDtypeStruct((n_batch, rows, LANES), F32)),
        grid_spec=pltpu.PrefetchScalarGridSpec(
            num_scalar_prefetch=1,
            grid=(n_batch, n_chunks),
            in_specs=[tok_spec, tok_spec,
                      pl.BlockSpec((SUBLANES, rows), lambda b, c, pt_ref: (0, 0))] + k_specs,
            out_specs=(pl.BlockSpec((1, n_chunks, rows, chunk), lambda b, c, pt_ref: (b, 0, 0, 0)),
                       row_spec, row_spec),
            scratch_shapes=[pltpu.VMEM((n_chunks, chunk, rows), F32),
                            pltpu.VMEM((n_blocks * N_HEADS, HEAD_DIM), F32),
                            pltpu.VMEM((d, rows), BF16),
                            pltpu.VMEM((n_blocks, rows), F32)]),
        compiler_params=_params("arbitrary", "arbitrary"),
        name="moba_decode_scores",
    )(pt, q3, kn3, slope_cols, *([ck] * len(k_specs)))

    v_specs = page_specs(VALUE_PAGES_PER_STEP)
    out = pl.pallas_call(
        functools.partial(_moba_decode_values_kernel, n_q=n_q),
        out_shape=jax.ShapeDtypeStruct((n_batch, n_q, d), F32),
        grid_spec=pltpu.PrefetchScalarGridSpec(
            num_scalar_prefetch=1,
            grid=(n_batch, n_pages // VALUE_PAGES_PER_STEP),
            in_specs=[pl.BlockSpec((1, chunks_per_value_step, rows, chunk),
                                   lambda b, c, pt_ref: (b, c, 0, 0)),
                      row_spec, row_spec, tok_spec, tok_spec] + v_specs,
            out_specs=tok_spec,
            scratch_shapes=[pltpu.VMEM((rows, d), F32)]),
        compiler_params=_params("arbitrary", "arbitrary"),
        name="moba_decode_values",
    )(pt, probs, p_own, denom, vn3, g3, *([cv] * len(v_specs)))
    return out.reshape(n_batch * n_q, d).astype(BF16)


def _cast_kernel(src_ref, dst_ref):
    dst_ref[...] = src_ref[...].astype(BF16)


def _cast_layer(w, layer, slab=128):
    _, r, c = w.shape
    return pl.pallas_call(
        _cast_kernel,
        out_shape=jax.ShapeDtypeStruct((1, r, c), BF16),
        grid=(r // slab,),
        in_specs=[pl.BlockSpec((None, slab, c), lambda i: (layer, i, 0))],
        out_specs=pl.BlockSpec((None, slab, c), lambda i: (0, i, 0)),
        compiler_params=_params("arbitrary"),
        name="cast_layer",
    )(w)


def _trunks(x_long, x_short, conv_prev_long, conv_prev_short, params, attend_long, attend_short):
    conv_w, ln_g, ln_b = params["conv_w"], params["ln_g"], params["ln_b"]
    n_l, t_l, d = x_long.shape
    n_s, t_s, _ = x_short.shape
    xl32, xs32 = x_long.reshape(n_l * t_l, d), x_short.reshape(n_s * t_s, d)
    xl16, xs16 = xl32.astype(BF16), xs32.astype(BF16)
    w16 = {("w_in_a", 0): _cast_layer(params["w_in_a"], 0)}

    def jobs(*names):
        return [(params[n] if params[n].ndim == 3 else params[n][None], l) for n, l in names]

    def keep(names, copies):
        w16.update(zip(names, copies))

    conv_l, conv_s = [], []
    ahead = [
        [("w_out_a", 0), ("w_in_a", 1)],
        [("w_out_a", 1), ("w_kv", 0), ("w_in_b", 0)],
    ]
    for l in range(N_A):
        yl16, rows_l, copies, (ys16, rows_s) = _conv_proj(
            xl16, w16["w_in_a", l], 0, conv_w[l], conv_prev_long[l], t_l, jobs(*ahead[l]),
            short=(xs16, conv_prev_short[l], t_s))
        keep(ahead[l], copies)
        conv_l.append(rows_l)
        conv_s.append(rows_s)
        xl32, xl16, xs32, xs16 = _out_proj_ln(yl16, w16["w_out_a", l], 0, xl32, ln_g[l], ln_b[l],
                                              short=(ys16, xs32))
    names = [("w_in_b", 1)]
    kl, vl, copies, (ks, vs) = _dual_proj(xl16, w16["w_kv", 0], 0, jobs(*names), short=xs16)
    keep(names, copies)
    for l in range(N_A, DEPTH):
        names = [("w_out_b", 0), ("w_out_b", 1)] if l == N_A else []
        ql, gl, copies, (qs, gs) = _dual_proj(xl16, w16["w_in_b", l - N_A], 0, jobs(*names), short=xs16)
        keep(names, copies)
        ol16 = attend_long(ql, gl, kl, vl)
        os16 = attend_short(qs, gs, ks, vs)
        xl32, xl16, xs32, xs16 = _out_proj_ln(ol16, w16["w_out_b", l - N_A], 0, xl32, ln_g[l], ln_b[l],
                                              short=(os16, xs32))
    heads_l, heads_s = (n_l, t_l, N_HEADS, HEAD_DIM), (n_s, t_s, N_HEADS, HEAD_DIM)
    return ((xl32.reshape(n_l, t_l, d), kl.reshape(heads_l), vl.reshape(heads_l), jnp.stack(conv_l)),
            (xs32.reshape(n_s, t_s, d), ks.reshape(heads_s), vs.reshape(heads_s), jnp.stack(conv_s)))


def kernel(x_prompt, x_sample, cache_k, cache_v, state_conv, page_table, w_in_a, conv_w, w_out_a,
           w_kv, w_in_b, w_out_b, ln_g, ln_b):
    params = dict(w_in_a=w_in_a, conv_w=conv_w, w_out_a=w_out_a, w_kv=w_kv, w_in_b=w_in_b,
                  w_out_b=w_out_b, ln_g=ln_g, ln_b=ln_b)
    n_prompt, seq, d = x_prompt.shape
    n_dec, dec_seq, _ = x_sample.shape
    zero_conv = jnp.zeros((N_A, n_prompt, CONV_W - 1, d), x_prompt.dtype)
    prompt, sample = _trunks(
        x_prompt, x_sample, zero_conv, state_conv, params,
        lambda q, g, k, v: _moba_prefill(q, k, v, g, n_prompt, seq),
        lambda q, g, k, v: _moba_decode(q, g, k, v, cache_k, cache_v, page_table, n_dec, dec_seq))
    y_prompt, k_prompt, v_prompt, conv_prompt = prompt
    y_sample, k_sample, v_sample, conv_sample = sample
    return (y_prompt, y_sample, k_prompt, v_prompt, conv_prompt, k_sample, v_sample, conv_sample)
```

```python
import functools
import math

import numpy as np
import jax
import jax.numpy as jnp
from jax import lax
from jax.experimental import pallas as pl
from jax.experimental.pallas import tpu as pltpu

N_HEADS = 32
HEAD_DIM = 128
CONV_W = 3
BLOCK = 256
TOP_K = 3
PAGE_SIZE = 128
DEPTH = 4
N_A = DEPTH // 2
ALPHA = (2.0 * DEPTH) ** 0.25
LN_EPS = 1e-5
SCALE = 1.0 / math.sqrt(HEAD_DIM)
LOG2_E = math.log2(math.e)

SUBLANES = 8
LANES = 128
VMEM_LIMIT_BYTES = 56 * 1024 * 1024
LN_VMEM_LIMIT_BYTES = 58 * 1024 * 1024
MASK_BIG = 2.0 ** 100
SCORE_PAGES_PER_STEP = 4
VALUE_PAGES_PER_STEP = 8
HEAD_GROUPS = N_HEADS // SUBLANES
ALIBI_PARTS = 4
HEADS_PER_STEP = 2

F32 = jnp.float32
BF16 = jnp.bfloat16


def _params(*semantics, vmem_limit=VMEM_LIMIT_BYTES):
    return pltpu.CompilerParams(dimension_semantics=semantics, vmem_limit_bytes=vmem_limit)


def _dot(a, b):
    return jnp.dot(a, b, preferred_element_type=F32)


def _dot_nt(a, b):
    return lax.dot_general(a, b, (((1,), (1,)), ((), ())), preferred_element_type=F32)


def _cast_jobs(jobs, n_steps, step_of):
    in_specs, out_specs, out_shapes, operands = [], [], [], []
    for w, layer in jobs:
        _, r, c = w.shape
        assert r % n_steps == 0 and (r // n_steps) % 16 == 0
        slab = r // n_steps
        in_specs.append(pl.BlockSpec(
            (None, slab, c), functools.partial(lambda *ids, layer: (layer, step_of(*ids), 0), layer=layer)))
        out_specs.append(pl.BlockSpec((None, slab, c), lambda *ids: (0, step_of(*ids), 0)))
        out_shapes.append(jax.ShapeDtypeStruct((1, r, c), BF16))
        operands.append(w)
    return in_specs, out_specs, out_shapes, operands


def _run_casts(src_refs, dst_refs):
    for src, dst in zip(src_refs, dst_refs):
        dst[...] = src[...].astype(BF16)


def _gated_conv(b_g, u, u_m1, u_m2, z, cw_ref):
    conv = cw_ref[0:1, :] * u_m2 + cw_ref[1:2, :] * u_m1 + cw_ref[2:3, :] * u
    return (b_g * conv) * (z * jax.nn.sigmoid(z))


def _conv_proj_kernel(x_ref, wb_ref, wc_ref, wh_ref, wz_ref, cw_ref, prev_ref, *rest,
                      tiles_per_seq, n_casts, short_len):
    n_short = 3 if short_len else 0
    short_in = rest[:n_short]
    cast_src = rest[n_short:n_short + n_casts]
    outs = rest[n_short + n_casts:]
    y_ref, tail_ref = outs[:2]
    short_out = outs[2:2 + (2 if short_len else 0)]
    cast_dst = outs[2 + len(short_out):2 + len(short_out) + n_casts]
    carry_ref = outs[-1]
    _run_casts(cast_src, cast_dst)
    m = pl.program_id(1)

    @pl.when(lax.rem(m, tiles_per_seq) == 0)
    def _():
        carry_ref[...] = prev_ref[0]

    def run(with_short):
        x = x_ref[...]
        tm = x.shape[0]
        if with_short:
            x = jnp.concatenate([x, short_in[0][...]], axis=0)
        b_all, c_all, h_all, z_all = (_dot(x, w[...]) for w in (wb_ref, wc_ref, wh_ref, wz_ref))
        u_all = c_all * h_all

        u = u_all[:tm]
        row = lax.broadcasted_iota(jnp.int32, u.shape, 0)
        c_m2 = carry_ref[SUBLANES - 2:SUBLANES - 1, :]
        c_m1 = carry_ref[SUBLANES - 1:SUBLANES, :]
        u_m1 = jnp.where(row == 0, c_m1, pltpu.roll(u, 1, 0))
        u_m2 = jnp.where(row == 0, c_m2, jnp.where(row == 1, c_m1, pltpu.roll(u, 2, 0)))
        y = _gated_conv(b_all[:tm], u, u_m1, u_m2, z_all[:tm], cw_ref)
        y_ref[...] = y.astype(y_ref.dtype)
        tail = u[tm - SUBLANES:, :]
        carry_ref[...] = tail
        tail_ref[0] = tail

        if with_short:
            _, pm1_ref, pm2_ref = short_in
            ys_ref, us_ref = short_out
            u_s = u_all[tm:]
            pos = lax.rem(lax.broadcasted_iota(jnp.int32, u_s.shape, 0), short_len)
            s_m1 = jnp.where(pos == 0, pm1_ref[...], pltpu.roll(u_s, 1, 0))
            s_m2 = jnp.where(pos < 2, pm2_ref[...], pltpu.roll(u_s, 2, 0))
            y_s = _gated_conv(b_all[tm:], u_s, s_m1, s_m2, z_all[tm:], cw_ref)
            ys_ref[...] = y_s.astype(ys_ref.dtype)
            us_ref[...] = u_s

    if short_len:
        pl.when(m == 0)(functools.partial(run, True))
        pl.when(m != 0)(functools.partial(run, False))
    else:
        run(False)


def _conv_proj(x16, w_in16, layer, conv_w, conv_prev, seq_len, casts=(), short=None):
    m_rows, d = x16.shape
    n_seq = m_rows // seq_len
    tn = 256
    nj = d // tn
    w_specs = [pl.BlockSpec((None, d, tn),
                            functools.partial(lambda j, m, g: (layer, 0, g * nj + j), g=g))
               for g in range(4)]
    tm = 1024 if seq_len % 1024 == 0 else 512
    assert seq_len % tm == 0
    tiles_per_seq = seq_len // tm
    nm = m_rows // tm
    prev8 = jnp.pad(conv_prev.astype(F32), ((0, 0), (SUBLANES - (CONV_W - 1), 0), (0, 0)))
    c_in, c_out, c_shapes, c_ops = _cast_jobs(casts, nj * nm, lambda j, m: j * nm + m)
    s_in, s_out, s_shapes, s_ops, short_len = [], [], [], [], 0
    if short is not None:
        xs16, prev_s, short_len = short
        ms = xs16.shape[0]
        n_s = ms // short_len
        assert short_len >= CONV_W - 1 and ms % 16 == 0
        prev_s = prev_s.astype(F32)
        zeros = jnp.zeros((n_s, short_len - 1, d), F32)
        pm1 = jnp.concatenate([prev_s[:, 1:2], zeros], axis=1).reshape(ms, d)
        pm2 = jnp.concatenate([prev_s, zeros[:, 1:]], axis=1).reshape(ms, d)
        col = pl.BlockSpec((ms, tn), lambda j, m: (0, j))
        s_in = [pl.BlockSpec((ms, d), lambda j, m: (0, 0)), col, col]
        s_out = [col, col]
        s_shapes = [jax.ShapeDtypeStruct((ms, d), BF16), jax.ShapeDtypeStruct((ms, d), F32)]
        s_ops = [xs16, pm1, pm2]
    outs = pl.pallas_call(
        functools.partial(_conv_proj_kernel, tiles_per_seq=tiles_per_seq, n_casts=len(casts),
                          short_len=short_len),
        out_shape=(jax.ShapeDtypeStruct((m_rows, d), BF16),
                   jax.ShapeDtypeStruct((n_seq, SUBLANES, d), F32), *s_shapes, *c_shapes),
        grid=(nj, nm),
        in_specs=[pl.BlockSpec((tm, d), lambda j, m: (m, 0))] + w_specs + [
            pl.BlockSpec((CONV_W, tn), lambda j, m: (0, j)),
            pl.BlockSpec((1, SUBLANES, tn), lambda j, m: (m // tiles_per_seq, 0, j))] + s_in + c_in,
        out_specs=(pl.BlockSpec((tm, tn), lambda j, m: (m, j)),
                   pl.BlockSpec((1, SUBLANES, tn), lambda j, m: (m // tiles_per_seq, 0, j)),
                   *s_out, *c_out),
        scratch_shapes=[pltpu.VMEM((SUBLANES, tn), F32)],
        compiler_params=_params("arbitrary", "arbitrary"),
        name="conv_proj",
    )(x16, w_in16, w_in16, w_in16, w_in16, conv_w, prev8, *s_ops, *c_ops)
    y, tail = outs[:2]
    short_res = None
    if short is not None:
        y_s, u_s = outs[2:4]
        short_res = (y_s, u_s.reshape(n_s, short_len, d)[:, short_len - (CONV_W - 1):, :])
    return y, tail[:, SUBLANES - (CONV_W - 1):, :], list(outs[2 + len(s_shapes):]), short_res


def _fold_lanes(v):
    out = v[:, :LANES]
    for t in range(1, v.shape[1] // LANES):
        out = out + v[:, t * LANES:(t + 1) * LANES]
    return out


def _ln_collect(r, j, nj, d, r_ref, shift_ref, sum_ref, sq_ref, mean_ref, rstd_ref):
    r_ref[j] = r

    @pl.when(j == 0)
    def _():
        shift_ref[...] = jnp.broadcast_to(jnp.mean(r, axis=-1, keepdims=True), shift_ref.shape)
        sum_ref[...] = jnp.zeros_like(sum_ref)
        sq_ref[...] = jnp.zeros_like(sq_ref)

    dlt = r - shift_ref[:, 0:1]
    sum_ref[...] += _fold_lanes(dlt)
    sq_ref[...] += _fold_lanes(dlt * dlt)

    @pl.when(j == nj - 1)
    def _():
        e1 = jnp.sum(sum_ref[...], axis=-1, keepdims=True) / d
        e2 = jnp.sum(sq_ref[...], axis=-1, keepdims=True) / d
        var = jnp.maximum(e2 - e1 * e1, 0.0)
        mean_ref[...] = jnp.broadcast_to(shift_ref[:, 0:1] + e1, mean_ref.shape)
        rstd_ref[...] = jnp.broadcast_to(lax.rsqrt(var + LN_EPS), rstd_ref.shape)


def _ln_emit(j, r_ref, mean_ref, rstd_ref, g_ref, b_ref, xo_ref, xo16_ref):
    o = (r_ref[j] - mean_ref[:, 0:1]) * rstd_ref[:, 0:1] * g_ref[...] + b_ref[...]
    xo_ref[...] = o
    xo16_ref[...] = o.astype(xo16_ref.dtype)


def _out_proj_ln_kernel(y_ref, w_ref, xres_ref, g_ref, b_ref, *rest, nm, nj, d, has_short):
    n_s = 2 if has_short else 0
    short_in = rest[:n_s]
    xo_ref, xo16_ref = rest[n_s:n_s + 2]
    short_out = rest[n_s + 2:2 * n_s + 2]
    scratch = rest[2 * n_s + 2:]
    main_state, short_state = scratch[:6], scratch[6:]
    mm = pl.program_id(0)
    j = pl.program_id(1)

    @pl.when(mm > 0)
    def _():
        _ln_emit(j, main_state[0], main_state[4], main_state[5], g_ref, b_ref, xo_ref, xo16_ref)

    def build(with_short):
        y = y_ref[...]
        tm = y.shape[0]
        if with_short:
            ys_ref, xres_s_ref = short_in
            y = jnp.concatenate([y, ys_ref[...]], axis=0)
        prod = _dot(y, w_ref[...])
        _ln_collect(ALPHA * xres_ref[...] + prod[:tm], j, nj, d, *main_state)
        if with_short:
            _ln_collect(ALPHA * xres_s_ref[...] + prod[tm:], j, nj, d, *short_state)

    if has_short:
        @pl.when(mm == 1)
        def _():
            _ln_emit(j, short_state[0], short_state[4], short_state[5], g_ref, b_ref, *short_out)

        pl.when(mm == 0)(functools.partial(build, True))
        pl.when((mm > 0) & (mm < nm))(functools.partial(build, False))
    else:
        pl.when(mm < nm)(functools.partial(build, False))


def _out_proj_ln(y16, w16, layer, x_res, gain, bias, short=None):
    m_rows, k = y16.shape
    d = w16.shape[2]
    tm = min(1024, m_rows)
    tn = 512
    assert m_rows % tm == 0 and d % tn == 0
    nj = d // tn
    nm = m_rows // tm
    row_in = lambda mm: jnp.minimum(mm, nm - 1)
    col_in = lambda mm, j: jnp.where(mm < nm, j, nj - 1)
    out_idx = lambda mm, j: (jnp.maximum(mm - 1, 0), jnp.where(mm == 0, 0, j))

    def state(rows):
        stat = pltpu.VMEM((rows, LANES), F32)
        return [pltpu.VMEM((nj, rows, tn), F32), stat, stat, stat, stat, stat]

    s_in, s_out, s_shapes, s_ops, s_state = [], [], [], [], []
    if short is not None:
        ys16, x_res_s = short
        ms = ys16.shape[0]
        s_in = [pl.BlockSpec((ms, k), lambda mm, j: (0, 0)),
                pl.BlockSpec((ms, tn), lambda mm, j: (0, jnp.where(mm == 0, j, nj - 1)))]
        s_idx = lambda mm, j: (0, jnp.where(mm == 0, 0, jnp.where(mm == 1, j, nj - 1)))
        s_out = [pl.BlockSpec((ms, tn), s_idx), pl.BlockSpec((ms, tn), s_idx)]
        s_shapes = [jax.ShapeDtypeStruct((ms, d), F32), jax.ShapeDtypeStruct((ms, d), BF16)]
        s_ops = [ys16, x_res_s]
        s_state = state(ms)
        assert nm >= 1
    return pl.pallas_call(
        functools.partial(_out_proj_ln_kernel, nm=nm, nj=nj, d=d, has_short=short is not None),
        out_shape=(jax.ShapeDtypeStruct((m_rows, d), F32), jax.ShapeDtypeStruct((m_rows, d), BF16),
                   *s_shapes),
        grid=(nm + 1, nj),
        in_specs=[pl.BlockSpec((tm, k), lambda mm, j: (row_in(mm), 0)),
                  pl.BlockSpec((None, k, tn), lambda mm, j: (layer, 0, col_in(mm, j))),
                  pl.BlockSpec((tm, tn), lambda mm, j: (row_in(mm), col_in(mm, j))),
                  pl.BlockSpec((1, tn), lambda mm, j: (0, j)),
                  pl.BlockSpec((1, tn), lambda mm, j: (0, j))] + s_in,
        out_specs=(pl.BlockSpec((tm, tn), out_idx), pl.BlockSpec((tm, tn), out_idx), *s_out),
        scratch_shapes=state(tm) + s_state,
        compiler_params=_params("arbitrary", "arbitrary", vmem_limit=LN_VMEM_LIMIT_BYTES),
        name="out_proj_ln",
    )(y16, w16, x_res, gain.reshape(1, d), bias.reshape(1, d), *s_ops)


def _dual_proj_kernel(x_ref, wa_ref, wb_ref, *rest, n_casts, has_short):
    n_s = 1 if has_short else 0
    cast_src = rest[n_s:n_s + n_casts]
    outs = rest[n_s + n_casts:]
    oa_ref, ob_ref = outs[:2]
    _run_casts(cast_src, outs[2 + 2 * n_s:])

    def run(with_short):
        x = x_ref[...]
        tm = x.shape[0]
        if with_short:
            x = jnp.concatenate([x, rest[0][...]], axis=0)
        prod_a = _dot(x, wa_ref[...])
        prod_b = _dot(x, wb_ref[...])
        oa_ref[...] = prod_a[:tm]
        ob_ref[...] = prod_b[:tm]
        if with_short:
            oas_ref, obs_ref = outs[2:4]
            oas_ref[...] = prod_a[tm:]
            obs_ref[...] = prod_b[tm:]

    if has_short:
        first = pl.program_id(0) == 0
        pl.when(first)(functools.partial(run, True))
        pl.when(jnp.logical_not(first))(functools.partial(run, False))
    else:
        run(False)


def _dual_proj(x16, w16, layer, casts=(), short=None):
    m_rows, k = x16.shape
    n_half = w16.shape[2] // 2
    tm = next(t for t in (1024, 512, m_rows) if m_rows % t == 0)
    tn = 512
    nj = n_half // tn
    out = jax.ShapeDtypeStruct((m_rows, n_half), F32)
    nm = m_rows // tm
    c_in, c_out, c_shapes, c_ops = _cast_jobs(casts, nm * nj, lambda m, j: m * nj + j)
    s_in, s_out, s_shapes, s_ops = [], [], [], []
    if short is not None:
        ms = short.shape[0]
        s_idx = lambda m, j: (0, jnp.where(m == 0, j, nj - 1))
        s_in = [pl.BlockSpec((ms, k), lambda m, j: (0, 0))]
        s_out = [pl.BlockSpec((ms, tn), s_idx), pl.BlockSpec((ms, tn), s_idx)]
        s_shapes = [jax.ShapeDtypeStruct((ms, n_half), F32)] * 2
        s_ops = [short]
    outs = pl.pallas_call(
        functools.partial(_dual_proj_kernel, n_casts=len(casts), has_short=short is not None),
        out_shape=(out, out, *s_shapes, *c_shapes),
        grid=(nm, nj),
        in_specs=[pl.BlockSpec((tm, k), lambda m, j: (m, 0)),
                  pl.BlockSpec((None, k, tn), lambda m, j: (layer, 0, j)),
                  pl.BlockSpec((None, k, tn), lambda m, j: (layer, 0, nj + j))] + s_in + c_in,
        out_specs=(pl.BlockSpec((tm, tn), lambda m, j: (m, j)),
                   pl.BlockSpec((tm, tn), lambda m, j: (m, j)), *s_out, *c_out),
        compiler_params=_params("arbitrary", "arbitrary"),
        name="dual_proj",
    )(x16, w16, w16, *s_ops, *c_ops)
    n_short = len(s_shapes)
    return outs[0], outs[1], list(outs[2 + n_short:]), tuple(outs[2:2 + n_short])


def _moba_prefill_kernel(q_ref, k_ref, v_ref, g_ref, kx_ref, qx_ref, o_ref, *scratch):
    seq = q_ref.shape[0]
    nb = seq // BLOCK
    per_head = 3 + nb
    r_i = lax.broadcasted_iota(jnp.int32, (BLOCK, BLOCK), 0)
    c_i = lax.broadcasted_iota(jnp.int32, (BLOCK, BLOCK), 1)
    causal = jnp.where(c_i <= r_i, 0.0, -jnp.inf)
    state = []
    for hh in range(HEADS_PER_STEP):
        qa_ref, ka_ref, vb_ref = scratch[hh * per_head:hh * per_head + 3]
        s_refs = scratch[hh * per_head + 3:(hh + 1) * per_head]
        cols = slice(hh * HEAD_DIM, (hh + 1) * HEAD_DIM)
        q16 = q_ref[:, cols].astype(BF16)
        k = k_ref[:, cols]
        kmean = jnp.mean(k.reshape(nb, BLOCK, HEAD_DIM), axis=1)
        gate_t = _dot_nt(kmean.astype(BF16), q16)
        blk = lax.broadcasted_iota(jnp.int32, (nb, seq), 0)
        q_blk = lax.broadcasted_iota(jnp.int32, (nb, seq), 1) // BLOCK
        rank = jnp.zeros((nb, seq), jnp.int32)
        for other in range(nb):
            g_o = gate_t[other:other + 1, :]
            beats = (g_o > gate_t) | ((g_o == gate_t) & (other < blk))
            rank += jnp.where((q_blk > other) & beats, 1, 0)
        dropped = ((blk < q_blk) & (rank >= TOP_K)).astype(F32)
        lane = lax.broadcasted_iota(jnp.int32, (nb, LANES), 1)
        spread = jnp.where(lane == lax.broadcasted_iota(jnp.int32, (nb, LANES), 0), -MASK_BIG, 0.0)
        sel_bias = lax.dot_general(dropped, spread, (((0,), (0,)), ((), ())),
                                   preferred_element_type=F32)
        qa_ref[:, :HEAD_DIM] = q16
        qa_ref[:, HEAD_DIM:] = (sel_bias + qx_ref[hh, 0:1, :]).astype(BF16)
        ka_ref[:, :HEAD_DIM] = k.astype(BF16)
        ka_ref[:, HEAD_DIM:] = kx_ref[...]
        vb_ref[:, :HEAD_DIM] = v_ref[:, cols].astype(BF16)
        ones_col = lax.broadcasted_iota(jnp.int32, (seq, HEAD_DIM), 1) == 0
        vb_ref[:, HEAD_DIM:] = jnp.where(ones_col, 1.0, 0.0).astype(BF16)
        state.append((qa_ref, ka_ref, vb_ref, s_refs, cols))
    for i in range(nb):
        for qa_ref, ka_ref, vb_ref, s_refs, cols in state:
            s_refs[i][...] = _dot_nt(qa_ref[i * BLOCK:, :], ka_ref[i * BLOCK:(i + 1) * BLOCK, :])
        for qa_ref, ka_ref, vb_ref, s_refs, cols in state:
            rows = slice(i * BLOCK, (i + 1) * BLOCK)
            n_keys = (i + 1) * BLOCK
            parts = [s_refs[n][(i - n) * BLOCK:(i - n + 1) * BLOCK, :] for n in range(i)]
            parts.append(s_refs[i][0:BLOCK, :] + causal)
            s = parts[0] if i == 0 else jnp.concatenate(parts, axis=1)
            m = jnp.max(s, axis=-1, keepdims=True)
            p = jnp.exp2((s - m) * (SCALE * LOG2_E))
            ol = _dot(p.astype(BF16), vb_ref[0:n_keys, :])
            o = ol[:, :HEAD_DIM] / ol[:, HEAD_DIM:HEAD_DIM + 1]
            gate = g_ref[rows, cols]
            o_ref[rows, cols] = (o * (gate * jax.nn.sigmoid(gate))).astype(o_ref.dtype)


def _alibi_slopes():
    return 2.0 ** (-8.0 * jnp.arange(1, N_HEADS + 1, dtype=F32) / N_HEADS)


def _moba_prefill(q, k, v, g, n_batch, seq):
    nb = seq // BLOCK
    assert nb <= SUBLANES and seq % BLOCK == 0
    pos = np.arange(seq)
    kx = np.zeros((seq, LANES), np.float32)
    kx[pos, pos // BLOCK] = 1.0
    kx[:, SUBLANES:SUBLANES + ALIBI_PARTS] = (pos % BLOCK)[:, None]
    kx[:, SUBLANES + ALIBI_PARTS:SUBLANES + 2 * ALIBI_PARTS] = ((pos // BLOCK) * BLOCK)[:, None]
    rest = _alibi_slopes() / SCALE
    pieces = []
    for _ in range(ALIBI_PARTS):
        piece = rest.astype(BF16).astype(F32)
        pieces.append(piece)
        rest = rest - piece
    pieces = jnp.stack(pieces, axis=1)
    qx = jnp.zeros((N_HEADS, LANES), F32)
    qx = qx.at[:, SUBLANES:SUBLANES + ALIBI_PARTS].set(pieces)
    qx = qx.at[:, SUBLANES + ALIBI_PARTS:SUBLANES + 2 * ALIBI_PARTS].set(pieces)
    qx = jnp.broadcast_to(qx[:, None, :], (N_HEADS, SUBLANES, LANES))
    pair_spec = pl.BlockSpec((seq, HEADS_PER_STEP * HEAD_DIM), lambda b, h: (b, h))
    wide = pltpu.VMEM((seq, 2 * HEAD_DIM), BF16)
    per_head = [wide, wide, wide] + [pltpu.VMEM(((nb - i) * BLOCK, BLOCK), F32) for i in range(nb)]
    return pl.pallas_call(
        _moba_prefill_kernel,
        out_shape=jax.ShapeDtypeStruct(q.shape, BF16),
        grid=(n_batch, N_HEADS // HEADS_PER_STEP),
        in_specs=[pair_spec, pair_spec, pair_spec, pair_spec,
                  pl.BlockSpec((seq, LANES), lambda b, h: (0, 0)),
                  pl.BlockSpec((HEADS_PER_STEP, SUBLANES, LANES), lambda b, h: (h, 0, 0))],
        out_specs=pair_spec,
        scratch_shapes=per_head * HEADS_PER_STEP,
        compiler_params=_params("arbitrary", "arbitrary"),
        name="moba_prefill",
    )(q, k, v, g, jnp.asarray(kx, BF16), qx)


def _block_diag_rows(x, n_q):
    rows = N_HEADS * n_q
    tiled = jnp.concatenate([x] * N_HEADS, axis=0)
    r_h = lax.broadcasted_iota(jnp.int32, (rows, x.shape[1]), 0) // n_q
    c_h = lax.broadcasted_iota(jnp.int32, (rows, x.shape[1]), 1) // HEAD_DIM
    return jnp.where(r_h == c_h, tiled, 0.0)


def _pad_rows(x, rows):
    return jnp.concatenate([x, jnp.zeros((rows - x.shape[0], x.shape[1]), x.dtype)], axis=0)


def _head_major(page_refs):
    n_pages = len(page_refs) // HEAD_GROUPS
    cols = []
    for h in range(N_HEADS):
        hg, sl = divmod(h, SUBLANES)
        rows = [page_refs[i * HEAD_GROUPS + hg].reshape(PAGE_SIZE * SUBLANES, HEAD_DIM)[
            pl.ds(sl, PAGE_SIZE, stride=SUBLANES), :] for i in range(n_pages)]
        cols.append(jnp.concatenate(rows, axis=0).astype(BF16))
    return jnp.concatenate(cols, axis=1)


def _moba_decode_scores_kernel(pt_ref, q_ref, knew_ref, slope_ref, *rest, n_q, past_len):
    n_in = SCORE_PAGES_PER_STEP * HEAD_GROUPS
    pages = rest[:n_in]
    p_ref, pown_ref, l_ref, s_ref, ksum_ref, qbd_ref, pick_ref = rest[n_in:]
    c = pl.program_id(1)
    n_chunks = pl.num_programs(1)
    chunk = SCORE_PAGES_PER_STEP * PAGE_SIZE
    blocks_per_chunk = chunk // BLOCK
    n_blocks = past_len // BLOCK
    rows = N_HEADS * n_q

    @pl.when(c == 0)
    def _():
        qbd_ref[...] = _block_diag_rows(q_ref[0], n_q).T.astype(BF16)

    pages_per_block = BLOCK // PAGE_SIZE
    for bi in range(blocks_per_chunk):
        for hg in range(HEAD_GROUPS):
            tot = None
            for i in range(bi * pages_per_block, (bi + 1) * pages_per_block):
                part = jnp.sum(pages[i * HEAD_GROUPS + hg][...], axis=0)
                tot = part if tot is None else tot + part
            row = ((c * blocks_per_chunk + bi) * HEAD_GROUPS + hg) * SUBLANES
            ksum_ref[pl.ds(pl.multiple_of(row, SUBLANES), SUBLANES), :] = tot
    s_ref[c] = _dot(_head_major(pages), qbd_ref[...])

    @pl.when(c == n_chunks - 1)
    def _():
        qbd = qbd_ref[...]
        ksum = jnp.concatenate([ksum_ref[pl.ds(h, n_blocks, stride=N_HEADS), :]
                                for h in range(N_HEADS)], axis=1)
        kmean = (ksum / BLOCK).astype(BF16)
        gate = _dot(kmean, qbd)
        idx = lax.broadcasted_iota(jnp.int32, gate.shape, 0)
        picked = jnp.zeros(gate.shape, jnp.bool_)
        for _ in range(TOP_K):
            best = jnp.max(gate, axis=0, keepdims=True)
            first = jnp.min(jnp.where(gate == best, idx, n_blocks), axis=0, keepdims=True)
            hit = idx == first
            picked = picked | hit
            gate = jnp.where(hit, -jnp.inf, gate)
        pick_ref[...] = picked.astype(F32)

        slope = slope_ref[0:1, :]
        q_i = lax.rem(lax.broadcasted_iota(jnp.int32, (1, rows), 1), n_q)
        t = (past_len + q_i).astype(F32)

        def mask_chunk(ci, m_run):
            pos = (ci * chunk + lax.broadcasted_iota(jnp.int32, (chunk, 1), 0)).astype(F32)
            s = s_ref[ci] * SCALE - slope * (t - pos)
            parts = []
            for bi in range(blocks_per_chunk):
                keep = pick_ref[pl.ds(ci * blocks_per_chunk + bi, 1), :]
                parts.append(jnp.where(keep > 0.5, s[bi * BLOCK:(bi + 1) * BLOCK], -jnp.inf))
            s = jnp.concatenate(parts, axis=0)
            s_ref[ci] = s
            return jnp.maximum(m_run, jnp.max(s, axis=0, keepdims=True))

        m_past = lax.fori_loop(0, n_chunks, mask_chunk, jnp.full((1, rows), -jnp.inf, F32))

        k_own = _pad_rows(knew_ref[0], LANES).astype(BF16)
        j_own = lax.broadcasted_iota(jnp.int32, (LANES, rows), 0)
        s_own = _dot(k_own, qbd) * SCALE - slope * (q_i - j_own).astype(F32)
        s_own = jnp.where(j_own <= q_i, s_own, -jnp.inf)
        m_all = jnp.maximum(m_past, jnp.max(s_own, axis=0, keepdims=True))
        p_own = jnp.exp(s_own - m_all)
        pown_ref[0] = p_own.T

        def exp_chunk(ci, l_run):
            p = jnp.exp(s_ref[ci] - m_all)
            p_ref[0, ci] = p.T.astype(p_ref.dtype)
            return l_run + jnp.sum(p, axis=0, keepdims=True)

        l_all = lax.fori_loop(0, n_chunks, exp_chunk, jnp.sum(p_own, axis=0, keepdims=True))
        l_ref[0] = jnp.broadcast_to(l_all, (LANES, rows)).T


def _moba_decode_values_kernel(pt_ref, p_ref, pown_ref, l_ref, vnew_ref, g_ref, *rest, n_q):
    n_in = VALUE_PAGES_PER_STEP * HEAD_GROUPS
    pages = rest[:n_in]
    o_ref, acc_ref = rest[n_in:]
    c = pl.program_id(1)
    pages_per_block = BLOCK // PAGE_SIZE
    blocks_per_chunk = SCORE_PAGES_PER_STEP // pages_per_block

    part = None
    for blk in range(VALUE_PAGES_PER_STEP // pages_per_block):
        slabs = pages[blk * pages_per_block * HEAD_GROUPS:(blk + 1) * pages_per_block * HEAD_GROUPS]
        sub, within = divmod(blk, blocks_per_chunk)
        term = _dot(p_ref[0, sub, :, within * BLOCK:(within + 1) * BLOCK], _head_major(slabs))
        part = term if part is None else part + term

    @pl.when(c == 0)
    def _():
        acc_ref[...] = part

    @pl.when(c > 0)
    def _():
        acc_ref[...] += part

    @pl.when(c == pl.num_programs(1) - 1)
    def _():
        v_own = _pad_rows(vnew_ref[0], LANES).astype(BF16)
        full = (acc_ref[...] + _dot(pown_ref[0].astype(BF16), v_own)) / l_ref[0][:, 0:1]
        r_h = lax.broadcasted_iota(jnp.int32, full.shape, 0) // n_q
        c_h = lax.broadcasted_iota(jnp.int32, full.shape, 1) // HEAD_DIM
        o = jnp.sum(jnp.where(r_h == c_h, full, 0.0).reshape(N_HEADS, n_q, full.shape[1]), axis=0)
        gate = g_ref[0]
        o_ref[0] = o * (gate * jax.nn.sigmoid(gate))


def _moba_decode(q, g, k_new, v_new, cache_k, cache_v, page_table, n_batch, n_q):
    d = q.shape[1]
    n_pages = page_table.shape[1]
    past_len = n_pages * PAGE_SIZE
    assert n_q == SUBLANES and past_len % BLOCK == 0 and n_pages % VALUE_PAGES_PER_STEP == 0
    assert VALUE_PAGES_PER_STEP % SCORE_PAGES_PER_STEP == 0
    n_chunks = n_pages // SCORE_PAGES_PER_STEP
    chunk = SCORE_PAGES_PER_STEP * PAGE_SIZE
    chunks_per_value_step = VALUE_PAGES_PER_STEP // SCORE_PAGES_PER_STEP
    n_blocks = past_len // BLOCK
    rows = N_HEADS * n_q
    pt = page_table.reshape(-1).astype(jnp.int32)
    ck = cache_k.reshape(cache_k.shape[0] * PAGE_SIZE, N_HEADS, HEAD_DIM)
    cv = cache_v.reshape(cache_v.shape[0] * PAGE_SIZE, N_HEADS, HEAD_DIM)
    q3, g3 = q.reshape(n_batch, n_q, d), g.reshape(n_batch, n_q, d)
    kn3, vn3 = k_new.reshape(n_batch, n_q, d), v_new.reshape(n_batch, n_q, d)
    slope_cols = jnp.broadcast_to(jnp.repeat(_alibi_slopes(), n_q)[None, :], (SUBLANES, rows))

    def page_specs(pages_per_step):
        def one(i, hg):
            return pl.BlockSpec(
                (PAGE_SIZE, SUBLANES, HEAD_DIM),
                lambda b, c, pt_ref: (pt_ref[b * n_pages + c * pages_per_step + i], hg, 0))
        return [one(i, hg) for i in range(pages_per_step) for hg in range(HEAD_GROUPS)]

    tok_spec = pl.BlockSpec((1, n_q, d), lambda b, c, pt_ref: (b, 0, 0))
    row_spec = pl.BlockSpec((1, rows, LANES), lambda b, c, pt_ref: (b, 0, 0))
    k_specs = page_specs(SCORE_PAGES_PER_STEP)
    probs, p_own, denom = pl.pallas_call(
        functools.partial(_moba_decode_scores_kernel, n_q=n_q, past_len=past_len),
        out_shape=(jax.ShapeDtypeStruct((n_batch, n_chunks, rows, chunk), BF16),
                   jax.ShapeDtypeStruct((n_batch, rows, LANES), F32),
                   jax.ShapeDtypeStruct((n_batch, rows, LANES), F32)),
        grid_spec=pltpu.PrefetchScalarGridSpec(
            num_scalar_prefetch=1,
            grid=(n_batch, n_chunks),
            in_specs=[tok_spec, tok_spec,
                      pl.BlockSpec((SUBLANES, rows), lambda b, c, pt_ref: (0, 0))] + k_specs,
            out_specs=(pl.BlockSpec((1, n_chunks, rows, chunk), lambda b, c, pt_ref: (b, 0, 0, 0)),
                       row_spec, row_spec),
            scratch_shapes=[pltpu.VMEM((n_chunks, chunk, rows), F32),
                            pltpu.VMEM((n_blocks * N_HEADS, HEAD_DIM), F32),
                            pltpu.VMEM((d, rows), BF16),
                            pltpu.VMEM((n_blocks, rows), F32)]),
        compiler_params=_params("arbitrary", "arbitrary"),
        name="moba_decode_scores",
    )(pt, q3, kn3, slope_cols, *([ck] * len(k_specs)))

    v_specs = page_specs(VALUE_PAGES_PER_STEP)
    out = pl.pallas_call(
        functools.partial(_moba_decode_values_kernel, n_q=n_q),
        out_shape=jax.ShapeDtypeStruct((n_batch, n_q, d), F32),
        grid_spec=pltpu.PrefetchScalarGridSpec(
            num_scalar_prefetch=1,
            grid=(n_batch, n_pages // VALUE_PAGES_PER_STEP),
            in_specs=[pl.BlockSpec((1, chunks_per_value_step, rows, chunk),
                                   lambda b, c, pt_ref: (b, c, 0, 0)),
                      row_spec, row_spec, tok_spec, tok_spec] + v_specs,
            out_specs=tok_spec,
            scratch_shapes=[pltpu.VMEM((rows, d), F32)]),
        compiler_params=_params("arbitrary", "arbitrary"),
        name="moba_decode_values",
    )(pt, probs, p_own, denom, vn3, g3, *([cv] * len(v_specs)))
    return out.reshape(n_batch * n_q, d).astype(BF16)


def _cast_kernel(src_ref, dst_ref):
    dst_ref[...] = src_ref[...].astype(BF16)


def _cast_layer(w, layer, slab=128):
    _, r, c = w.shape
    return pl.pallas_call(
        _cast_kernel,
        out_shape=jax.ShapeDtypeStruct((1, r, c), BF16),
        grid=(r // slab,),
        in_specs=[pl.BlockSpec((None, slab, c), lambda i: (layer, i, 0))],
        out_specs=pl.BlockSpec((None, slab, c), lambda i: (0, i, 0)),
        compiler_params=_params("arbitrary"),
        name="cast_layer",
    )(w)


def _trunks(x_long, x_short, conv_prev_long, conv_prev_short, params, attend_long, attend_short):
    conv_w, ln_g, ln_b = params["conv_w"], params["ln_g"], params["ln_b"]
    n_l, t_l, d = x_long.shape
    n_s, t_s, _ = x_short.shape
    xl32, xs32 = x_long.reshape(n_l * t_l, d), x_short.reshape(n_s * t_s, d)
    xl16, xs16 = xl32.astype(BF16), xs32.astype(BF16)
    w16 = {("w_in_a", 0): _cast_layer(params["w_in_a"], 0)}

    def jobs(*names):
        return [(params[n] if params[n].ndim == 3 else params[n][None], l) for n, l in names]

    def keep(names, copies):
        w16.update(zip(names, copies))

    conv_l, conv_s = [], []
    ahead = [
        [("w_out_a", 0), ("w_in_a", 1), ("w_in_b", 1)],
        [("w_out_a", 1), ("w_kv", 0), ("w_in_b", 0), ("w_out_b", 0), ("w_out_b", 1)],
    ]
    for l in range(N_A):
        yl16, rows_l, copies, (ys16, rows_s) = _conv_proj(
            xl16, w16["w_in_a", l], 0, conv_w[l], conv_prev_long[l], t_l, jobs(*ahead[l]),
            short=(xs16, conv_prev_short[l], t_s))
        keep(ahead[l], copies)
        conv_l.append(rows_l)
        conv_s.append(rows_s)
        xl32, xl16, xs32, xs16 = _out_proj_ln(yl16, w16["w_out_a", l], 0, xl32, ln_g[l], ln_b[l],
                                              short=(ys16, xs32))
    kl, vl, _, (ks, vs) = _dual_proj(xl16, w16["w_kv", 0], 0, short=xs16)
    for l in range(N_A, DEPTH):
        ql, gl, _, (qs, gs) = _dual_proj(xl16, w16["w_in_b", l - N_A], 0, short=xs16)
        ol16 = attend_long(ql, gl, kl, vl)
        os16 = attend_short(qs, gs, ks, vs)
        xl32, xl16, xs32, xs16 = _out_proj_ln(ol16, w16["w_out_b", l - N_A], 0, xl32, ln_g[l], ln_b[l],
                                              short=(os16, xs32))
    heads_l, heads_s = (n_l, t_l, N_HEADS, HEAD_DIM), (n_s, t_s, N_HEADS, HEAD_DIM)
    return ((xl32.reshape(n_l, t_l, d), kl.reshape(heads_l), vl.reshape(heads_l), jnp.stack(conv_l)),
            (xs32.reshape(n_s, t_s, d), ks.reshape(heads_s), vs.reshape(heads_s), jnp.stack(conv_s)))


def kernel(x_prompt, x_sample, cache_k, cache_v, state_conv, page_table, w_in_a, conv_w, w_out_a,
           w_kv, w_in_b, w_out_b, ln_g, ln_b):
    params = dict(w_in_a=w_in_a, conv_w=conv_w, w_out_a=w_out_a, w_kv=w_kv, w_in_b=w_in_b,
                  w_out_b=w_out_b, ln_g=ln_g, ln_b=ln_b)
    n_prompt, seq, d = x_prompt.shape
    n_dec, dec_seq, _ = x_sample.shape
    zero_conv = jnp.zeros((N_A, n_prompt, CONV_W - 1, d), x_prompt.dtype)
    prompt, sample = _trunks(
        x_prompt, x_sample, zero_conv, state_conv, params,
        lambda q, g, k, v: _moba_prefill(q, k, v, g, n_prompt, seq),
        lambda q, g, k, v: _moba_decode(q, g, k, v, cache_k, cache_v, page_table, n_dec, dec_seq))
    y_prompt, k_prompt, v_prompt, conv_prompt = prompt
    y_sample, k_sample, v_sample, conv_sample = sample
    return (y_prompt, y_sample, k_prompt, v_prompt, conv_prompt, k_sample, v_sample, conv_sample)
```

```python
import functools
import math

import numpy as np
import jax
import jax.numpy as jnp
from jax import lax
from jax.experimental import pallas as pl
from jax.experimental.pallas import tpu as pltpu

N_HEADS = 32
HEAD_DIM = 128
CONV_W = 3
BLOCK = 256
TOP_K = 3
PAGE_SIZE = 128
DEPTH = 4
N_A = DEPTH // 2
ALPHA = (2.0 * DEPTH) ** 0.25
LN_EPS = 1e-5
SCALE = 1.0 / math.sqrt(HEAD_DIM)
LOG2_E = math.log2(math.e)

SUBLANES = 8
LANES = 128
VMEM_LIMIT_BYTES = 56 * 1024 * 1024
LN_VMEM_LIMIT_BYTES = 58 * 1024 * 1024
MASK_BIG = 2.0 ** 100
SCORE_PAGES_PER_STEP = 4
VALUE_PAGES_PER_STEP = 8
HEAD_GROUPS = N_HEADS // SUBLANES
ALIBI_PARTS = 4
HEADS_PER_STEP = 2

F32 = jnp.float32
BF16 = jnp.bfloat16


def _params(*semantics, vmem_limit=VMEM_LIMIT_BYTES):
    return pltpu.CompilerParams(dimension_semantics=semantics, vmem_limit_bytes=vmem_limit)


def _dot(a, b):
    return jnp.dot(a, b, preferred_element_type=F32)


def _dot_nt(a, b):
    return lax.dot_general(a, b, (((1,), (1,)), ((), ())), preferred_element_type=F32)


def _cast_jobs(jobs, n_steps, step_of):
    in_specs, out_specs, out_shapes, operands = [], [], [], []
    for w, layer in jobs:
        _, r, c = w.shape
        assert r % n_steps == 0 and (r // n_steps) % 16 == 0
        slab = r // n_steps
        in_specs.append(pl.BlockSpec(
            (None, slab, c), functools.partial(lambda *ids, layer: (layer, step_of(*ids), 0), layer=layer)))
        out_specs.append(pl.BlockSpec((None, slab, c), lambda *ids: (0, step_of(*ids), 0)))
        out_shapes.append(jax.ShapeDtypeStruct((1, r, c), BF16))
        operands.append(w)
    return in_specs, out_specs, out_shapes, operands


def _run_casts(src_refs, dst_refs):
    for src, dst in zip(src_refs, dst_refs):
        dst[...] = src[...].astype(BF16)


def _gated_conv(b_g, u, u_m1, u_m2, z, cw_ref):
    conv = cw_ref[0:1, :] * u_m2 + cw_ref[1:2, :] * u_m1 + cw_ref[2:3, :] * u
    return (b_g * conv) * (z * jax.nn.sigmoid(z))


def _conv_proj_kernel(x_ref, wb_ref, wc_ref, wh_ref, wz_ref, cw_ref, prev_ref, *rest,
                      tiles_per_seq, n_casts, short_len):
    n_short = 3 if short_len else 0
    short_in = rest[:n_short]
    cast_src = rest[n_short:n_short + n_casts]
    outs = rest[n_short + n_casts:]
    y_ref, tail_ref = outs[:2]
    short_out = outs[2:2 + (2 if short_len else 0)]
    cast_dst = outs[2 + len(short_out):2 + len(short_out) + n_casts]
    carry_ref = outs[-1]
    _run_casts(cast_src, cast_dst)
    m = pl.program_id(1)

    @pl.when(lax.rem(m, tiles_per_seq) == 0)
    def _():
        carry_ref[...] = prev_ref[0]

    def run(with_short):
        x = x_ref[...]
        tm = x.shape[0]
        if with_short:
            x = jnp.concatenate([x, short_in[0][...]], axis=0)
        b_all, c_all, h_all, z_all = (_dot(x, w[...]) for w in (wb_ref, wc_ref, wh_ref, wz_ref))
        u_all = c_all * h_all

        u = u_all[:tm]
        row = lax.broadcasted_iota(jnp.int32, u.shape, 0)
        c_m2 = carry_ref[SUBLANES - 2:SUBLANES - 1, :]
        c_m1 = carry_ref[SUBLANES - 1:SUBLANES, :]
        u_m1 = jnp.where(row == 0, c_m1, pltpu.roll(u, 1, 0))
        u_m2 = jnp.where(row == 0, c_m2, jnp.where(row == 1, c_m1, pltpu.roll(u, 2, 0)))
        y = _gated_conv(b_all[:tm], u, u_m1, u_m2, z_all[:tm], cw_ref)
        y_ref[...] = y.astype(y_ref.dtype)
        tail = u[tm - SUBLANES:, :]
        carry_ref[...] = tail
        tail_ref[0] = tail

        if with_short:
            _, pm1_ref, pm2_ref = short_in
            ys_ref, us_ref = short_out
            u_s = u_all[tm:]
            pos = lax.rem(lax.broadcasted_iota(jnp.int32, u_s.shape, 0), short_len)
            s_m1 = jnp.where(pos == 0, pm1_ref[...], pltpu.roll(u_s, 1, 0))
            s_m2 = jnp.where(pos < 2, pm2_ref[...], pltpu.roll(u_s, 2, 0))
            y_s = _gated_conv(b_all[tm:], u_s, s_m1, s_m2, z_all[tm:], cw_ref)
            ys_ref[...] = y_s.astype(ys_ref.dtype)
            us_ref[...] = u_s

    if short_len:
        pl.when(m == 0)(functools.partial(run, True))
        pl.when(m != 0)(functools.partial(run, False))
    else:
        run(False)


def _conv_proj(x16, w_in16, layer, conv_w, conv_prev, seq_len, casts=(), short=None):
    m_rows, d = x16.shape
    n_seq = m_rows // seq_len
    tn = 256
    nj = d // tn
    w_specs = [pl.BlockSpec((None, d, tn),
                            functools.partial(lambda j, m, g: (layer, 0, g * nj + j), g=g))
               for g in range(4)]
    tm = 1024 if seq_len % 1024 == 0 else 512
    assert seq_len % tm == 0
    tiles_per_seq = seq_len // tm
    nm = m_rows // tm
    prev8 = jnp.pad(conv_prev.astype(F32), ((0, 0), (SUBLANES - (CONV_W - 1), 0), (0, 0)))
    c_in, c_out, c_shapes, c_ops = _cast_jobs(casts, nj * nm, lambda j, m: j * nm + m)
    s_in, s_out, s_shapes, s_ops, short_len = [], [], [], [], 0
    if short is not None:
        xs16, prev_s, short_len = short
        ms = xs16.shape[0]
        n_s = ms // short_len
        assert short_len >= CONV_W - 1 and ms % 16 == 0
        prev_s = prev_s.astype(F32)
        zeros = jnp.zeros((n_s, short_len - 1, d), F32)
        pm1 = jnp.concatenate([prev_s[:, 1:2], zeros], axis=1).reshape(ms, d)
        pm2 = jnp.concatenate([prev_s, zeros[:, 1:]], axis=1).reshape(ms, d)
        col = pl.BlockSpec((ms, tn), lambda j, m: (0, j))
        s_in = [pl.BlockSpec((ms, d), lambda j, m: (0, 0)), col, col]
        s_out = [col, col]
        s_shapes = [jax.ShapeDtypeStruct((ms, d), BF16), jax.ShapeDtypeStruct((ms, d), F32)]
        s_ops = [xs16, pm1, pm2]
    outs = pl.pallas_call(
        functools.partial(_conv_proj_kernel, tiles_per_seq=tiles_per_seq, n_casts=len(casts),
                          short_len=short_len),
        out_shape=(jax.ShapeDtypeStruct((m_rows, d), BF16),
                   jax.ShapeDtypeStruct((n_seq, SUBLANES, d), F32), *s_shapes, *c_shapes),
        grid=(nj, nm),
        in_specs=[pl.BlockSpec((tm, d), lambda j, m: (m, 0))] + w_specs + [
            pl.BlockSpec((CONV_W, tn), lambda j, m: (0, j)),
            pl.BlockSpec((1, SUBLANES, tn), lambda j, m: (m // tiles_per_seq, 0, j))] + s_in + c_in,
        out_specs=(pl.BlockSpec((tm, tn), lambda j, m: (m, j)),
                   pl.BlockSpec((1, SUBLANES, tn), lambda j, m: (m // tiles_per_seq, 0, j)),
                   *s_out, *c_out),
        scratch_shapes=[pltpu.VMEM((SUBLANES, tn), F32)],
        compiler_params=_params("arbitrary", "arbitrary"),
        name="conv_proj",
    )(x16, w_in16, w_in16, w_in16, w_in16, conv_w, prev8, *s_ops, *c_ops)
    y, tail = outs[:2]
    short_res = None
    if short is not None:
        y_s, u_s = outs[2:4]
        short_res = (y_s, u_s.reshape(n_s, short_len, d)[:, short_len - (CONV_W - 1):, :])
    return y, tail[:, SUBLANES - (CONV_W - 1):, :], list(outs[2 + len(s_shapes):]), short_res


def _fold_lanes(v):
    out = v[:, :LANES]
    for t in range(1, v.shape[1] // LANES):
        out = out + v[:, t * LANES:(t + 1) * LANES]
    return out


def _ln_collect(r, j, nj, d, r_ref, shift_ref, sum_ref, sq_ref, mean_ref, rstd_ref):
    r_ref[j] = r

    @pl.when(j == 0)
    def _():
        shift_ref[...] = jnp.broadcast_to(jnp.mean(r, axis=-1, keepdims=True), shift_ref.shape)
        sum_ref[...] = jnp.zeros_like(sum_ref)
        sq_ref[...] = jnp.zeros_like(sq_ref)

    dlt = r - shift_ref[:, 0:1]
    sum_ref[...] += _fold_lanes(dlt)
    sq_ref[...] += _fold_lanes(dlt * dlt)

    @pl.when(j == nj - 1)
    def _():
        e1 = jnp.sum(sum_ref[...], axis=-1, keepdims=True) / d
        e2 = jnp.sum(sq_ref[...], axis=-1, keepdims=True) / d
        var = jnp.maximum(e2 - e1 * e1, 0.0)
        mean_ref[...] = jnp.broadcast_to(shift_ref[:, 0:1] + e1, mean_ref.shape)
        rstd_ref[...] = jnp.broadcast_to(lax.rsqrt(var + LN_EPS), rstd_ref.shape)


def _ln_emit(j, r_ref, mean_ref, rstd_ref, g_ref, b_ref, xo_ref, xo16_ref):
    o = (r_ref[j] - mean_ref[:, 0:1]) * rstd_ref[:, 0:1] * g_ref[...] + b_ref[...]
    xo_ref[...] = o
    xo16_ref[...] = o.astype(xo16_ref.dtype)


def _out_proj_ln_kernel(y_ref, w_ref, xres_ref, g_ref, b_ref, *rest, nm, nj, d, has_short):
    n_s = 2 if has_short else 0
    short_in = rest[:n_s]
    xo_ref, xo16_ref = rest[n_s:n_s + 2]
    short_out = rest[n_s + 2:2 * n_s + 2]
    scratch = rest[2 * n_s + 2:]
    main_state, short_state = scratch[:6], scratch[6:]
    mm = pl.program_id(0)
    j = pl.program_id(1)

    def emit():
        _ln_emit(j, main_state[0], main_state[4], main_state[5], g_ref, b_ref, xo_ref, xo16_ref)

    def build(with_short):
        y = y_ref[...]
        tm = y.shape[0]
        if with_short:
            ys_ref, xres_s_ref = short_in
            y = jnp.concatenate([y, ys_ref[...]], axis=0)
        prod = _dot(y, w_ref[...])
        _ln_collect(ALPHA * xres_ref[...] + prod[:tm], j, nj, d, *main_state)
        if with_short:
            _ln_collect(ALPHA * xres_s_ref[...] + prod[tm:], j, nj, d, *short_state)

    def emit_and_build():
        emit()
        build(False)

    if has_short:
        @pl.when(mm == 1)
        def _():
            _ln_emit(j, short_state[0], short_state[4], short_state[5], g_ref, b_ref, *short_out)

    pl.when(mm == 0)(functools.partial(build, has_short))
    pl.when((mm > 0) & (mm < nm))(emit_and_build)
    pl.when(mm == nm)(emit)


def _out_proj_ln(y16, w16, layer, x_res, gain, bias, short=None):
    m_rows, k = y16.shape
    d = w16.shape[2]
    tm = min(1024, m_rows)
    tn = 512
    assert m_rows % tm == 0 and d % tn == 0
    nj = d // tn
    nm = m_rows // tm
    row_in = lambda mm: jnp.minimum(mm, nm - 1)
    col_in = lambda mm, j: jnp.where(mm < nm, j, nj - 1)
    out_idx = lambda mm, j: (jnp.maximum(mm - 1, 0), jnp.where(mm == 0, 0, j))

    def state(rows):
        stat = pltpu.VMEM((rows, LANES), F32)
        return [pltpu.VMEM((nj, rows, tn), F32), stat, stat, stat, stat, stat]

    s_in, s_out, s_shapes, s_ops, s_state = [], [], [], [], []
    if short is not None:
        ys16, x_res_s = short
        ms = ys16.shape[0]
        s_in = [pl.BlockSpec((ms, k), lambda mm, j: (0, 0)),
                pl.BlockSpec((ms, tn), lambda mm, j: (0, jnp.where(mm == 0, j, nj - 1)))]
        s_idx = lambda mm, j: (0, jnp.where(mm == 0, 0, jnp.where(mm == 1, j, nj - 1)))
        s_out = [pl.BlockSpec((ms, tn), s_idx), pl.BlockSpec((ms, tn), s_idx)]
        s_shapes = [jax.ShapeDtypeStruct((ms, d), F32), jax.ShapeDtypeStruct((ms, d), BF16)]
        s_ops = [ys16, x_res_s]
        s_state = state(ms)
        assert nm >= 1
    return pl.pallas_call(
        functools.partial(_out_proj_ln_kernel, nm=nm, nj=nj, d=d, has_short=short is not None),
        out_shape=(jax.ShapeDtypeStruct((m_rows, d), F32), jax.ShapeDtypeStruct((m_rows, d), BF16),
                   *s_shapes),
        grid=(nm + 1, nj),
        in_specs=[pl.BlockSpec((tm, k), lambda mm, j: (row_in(mm), 0)),
                  pl.BlockSpec((None, k, tn), lambda mm, j: (layer, 0, col_in(mm, j))),
                  pl.BlockSpec((tm, tn), lambda mm, j: (row_in(mm), col_in(mm, j))),
                  pl.BlockSpec((1, tn), lambda mm, j: (0, j)),
                  pl.BlockSpec((1, tn), lambda mm, j: (0, j))] + s_in,
        out_specs=(pl.BlockSpec((tm, tn), out_idx), pl.BlockSpec((tm, tn), out_idx), *s_out),
        scratch_shapes=state(tm) + s_state,
        compiler_params=_params("arbitrary", "arbitrary", vmem_limit=LN_VMEM_LIMIT_BYTES),
        name="out_proj_ln",
    )(y16, w16, x_res, gain.reshape(1, d), bias.reshape(1, d), *s_ops)


def _dual_proj_kernel(x_ref, wa_ref, wb_ref, *rest, n_casts, has_short):
    n_s = 1 if has_short else 0
    cast_src = rest[n_s:n_s + n_casts]
    outs = rest[n_s + n_casts:]
    oa_ref, ob_ref = outs[:2]
    _run_casts(cast_src, outs[2 + 2 * n_s:])

    def run(with_short):
        x = x_ref[...]
        tm = x.shape[0]
        if with_short:
            x = jnp.concatenate([x, rest[0][...]], axis=0)
        prod_a = _dot(x, wa_ref[...])
        prod_b = _dot(x, wb_ref[...])
        oa_ref[...] = prod_a[:tm]
        ob_ref[...] = prod_b[:tm]
        if with_short:
            oas_ref, obs_ref = outs[2:4]
            oas_ref[...] = prod_a[tm:]
            obs_ref[...] = prod_b[tm:]

    if has_short:
        first = pl.program_id(0) == 0
        pl.when(first)(functools.partial(run, True))
        pl.when(jnp.logical_not(first))(functools.partial(run, False))
    else:
        run(False)


def _dual_proj(x16, w16, layer, casts=(), short=None):
    m_rows, k = x16.shape
    n_half = w16.shape[2] // 2
    tm = next(t for t in (1024, 512, m_rows) if m_rows % t == 0)
    tn = 512
    nj = n_half // tn
    out = jax.ShapeDtypeStruct((m_rows, n_half), F32)
    nm = m_rows // tm
    c_in, c_out, c_shapes, c_ops = _cast_jobs(casts, nm * nj, lambda m, j: m * nj + j)
    s_in, s_out, s_shapes, s_ops = [], [], [], []
    if short is not None:
        ms = short.shape[0]
        s_idx = lambda m, j: (0, jnp.where(m == 0, j, nj - 1))
        s_in = [pl.BlockSpec((ms, k), lambda m, j: (0, 0))]
        s_out = [pl.BlockSpec((ms, tn), s_idx), pl.BlockSpec((ms, tn), s_idx)]
        s_shapes = [jax.ShapeDtypeStruct((ms, n_half), F32)] * 2
        s_ops = [short]
    outs = pl.pallas_call(
        functools.partial(_dual_proj_kernel, n_casts=len(casts), has_short=short is not None),
        out_shape=(out, out, *s_shapes, *c_shapes),
        grid=(nm, nj),
        in_specs=[pl.BlockSpec((tm, k), lambda m, j: (m, 0)),
                  pl.BlockSpec((None, k, tn), lambda m, j: (layer, 0, j)),
                  pl.BlockSpec((None, k, tn), lambda m, j: (layer, 0, nj + j))] + s_in + c_in,
        out_specs=(pl.BlockSpec((tm, tn), lambda m, j: (m, j)),
                   pl.BlockSpec((tm, tn), lambda m, j: (m, j)), *s_out, *c_out),
        compiler_params=_params("arbitrary", "arbitrary"),
        name="dual_proj",
    )(x16, w16, w16, *s_ops, *c_ops)
    n_short = len(s_shapes)
    return outs[0], outs[1], list(outs[2 + n_short:]), tuple(outs[2:2 + n_short])


def _moba_prefill_kernel(q_ref, k_ref, v_ref, g_ref, kx_ref, qx_ref, o_ref, *scratch):
    seq = q_ref.shape[0]
    nb = seq // BLOCK
    per_head = 3 + nb
    r_i = lax.broadcasted_iota(jnp.int32, (BLOCK, BLOCK), 0)
    c_i = lax.broadcasted_iota(jnp.int32, (BLOCK, BLOCK), 1)
    causal = jnp.where(c_i <= r_i, 0.0, -jnp.inf)
    state = []
    for hh in range(HEADS_PER_STEP):
        qa_ref, ka_ref, vb_ref = scratch[hh * per_head:hh * per_head + 3]
        s_refs = scratch[hh * per_head + 3:(hh + 1) * per_head]
        cols = slice(hh * HEAD_DIM, (hh + 1) * HEAD_DIM)
        q16 = q_ref[:, cols].astype(BF16)
        k = k_ref[:, cols]
        kmean = jnp.mean(k.reshape(nb, BLOCK, HEAD_DIM), axis=1)
        gate_t = _dot_nt(kmean.astype(BF16), q16)
        blk = lax.broadcasted_iota(jnp.int32, (nb, seq), 0)
        q_blk = lax.broadcasted_iota(jnp.int32, (nb, seq), 1) // BLOCK
        rank = jnp.zeros((nb, seq), jnp.int32)
        for other in range(nb):
            g_o = gate_t[other:other + 1, :]
            beats = (g_o > gate_t) | ((g_o == gate_t) & (other < blk))
            rank += jnp.where((q_blk > other) & beats, 1, 0)
        dropped = ((blk < q_blk) & (rank >= TOP_K)).astype(F32)
        lane = lax.broadcasted_iota(jnp.int32, (nb, LANES), 1)
        spread = jnp.where(lane == lax.broadcasted_iota(jnp.int32, (nb, LANES), 0), -MASK_BIG, 0.0)
        sel_bias = lax.dot_general(dropped, spread, (((0,), (0,)), ((), ())),
                                   preferred_element_type=F32)
        qa_ref[:, :HEAD_DIM] = q16
        qa_ref[:, HEAD_DIM:] = (sel_bias + qx_ref[hh, 0:1, :]).astype(BF16)
        ka_ref[:, :HEAD_DIM] = k.astype(BF16)
        ka_ref[:, HEAD_DIM:] = kx_ref[...]
        vb_ref[:, :HEAD_DIM] = v_ref[:, cols].astype(BF16)
        ones_col = lax.broadcasted_iota(jnp.int32, (seq, HEAD_DIM), 1) == 0
        vb_ref[:, HEAD_DIM:] = jnp.where(ones_col, 1.0, 0.0).astype(BF16)
        state.append((qa_ref, ka_ref, vb_ref, s_refs, cols))
    for i in range(nb):
        for qa_ref, ka_ref, vb_ref, s_refs, cols in state:
            s_refs[i][...] = _dot_nt(qa_ref[i * BLOCK:, :], ka_ref[i * BLOCK:(i + 1) * BLOCK, :])
        for qa_ref, ka_ref, vb_ref, s_refs, cols in state:
            rows = slice(i * BLOCK, (i + 1) * BLOCK)
            n_keys = (i + 1) * BLOCK
            parts = [s_refs[n][(i - n) * BLOCK:(i - n + 1) * BLOCK, :] for n in range(i)]
            parts.append(s_refs[i][0:BLOCK, :] + causal)
            s = parts[0] if i == 0 else jnp.concatenate(parts, axis=1)
            m = jnp.max(s, axis=-1, keepdims=True)
            p = jnp.exp2((s - m) * (SCALE * LOG2_E))
            ol = _dot(p.astype(BF16), vb_ref[0:n_keys, :])
            o = ol[:, :HEAD_DIM] / ol[:, HEAD_DIM:HEAD_DIM + 1]
            gate = g_ref[rows, cols]
            o_ref[rows, cols] = (o * (gate * jax.nn.sigmoid(gate))).astype(o_ref.dtype)


def _alibi_slopes():
    return 2.0 ** (-8.0 * jnp.arange(1, N_HEADS + 1, dtype=F32) / N_HEADS)


def _moba_prefill(q, k, v, g, n_batch, seq):
    nb = seq // BLOCK
    assert nb <= SUBLANES and seq % BLOCK == 0
    pos = np.arange(seq)
    kx = np.zeros((seq, LANES), np.float32)
    kx[pos, pos // BLOCK] = 1.0
    kx[:, SUBLANES:SUBLANES + ALIBI_PARTS] = (pos % BLOCK)[:, None]
    kx[:, SUBLANES + ALIBI_PARTS:SUBLANES + 2 * ALIBI_PARTS] = ((pos // BLOCK) * BLOCK)[:, None]
    rest = _alibi_slopes() / SCALE
    pieces = []
    for _ in range(ALIBI_PARTS):
        piece = rest.astype(BF16).astype(F32)
        pieces.append(piece)
        rest = rest - piece
    pieces = jnp.stack(pieces, axis=1)
    qx = jnp.zeros((N_HEADS, LANES), F32)
    qx = qx.at[:, SUBLANES:SUBLANES + ALIBI_PARTS].set(pieces)
    qx = qx.at[:, SUBLANES + ALIBI_PARTS:SUBLANES + 2 * ALIBI_PARTS].set(pieces)
    qx = jnp.broadcast_to(qx[:, None, :], (N_HEADS, SUBLANES, LANES))
    pair_spec = pl.BlockSpec((seq, HEADS_PER_STEP * HEAD_DIM), lambda b, h: (b, h))
    wide = pltpu.VMEM((seq, 2 * HEAD_DIM), BF16)
    per_head = [wide, wide, wide] + [pltpu.VMEM(((nb - i) * BLOCK, BLOCK), F32) for i in range(nb)]
    return pl.pallas_call(
        _moba_prefill_kernel,
        out_shape=jax.ShapeDtypeStruct(q.shape, BF16),
        grid=(n_batch, N_HEADS // HEADS_PER_STEP),
        in_specs=[pair_spec, pair_spec, pair_spec, pair_spec,
                  pl.BlockSpec((seq, LANES), lambda b, h: (0, 0)),
                  pl.BlockSpec((HEADS_PER_STEP, SUBLANES, LANES), lambda b, h: (h, 0, 0))],
        out_specs=pair_spec,
        scratch_shapes=per_head * HEADS_PER_STEP,
        compiler_params=_params("arbitrary", "arbitrary"),
        name="moba_prefill",
    )(q, k, v, g, jnp.asarray(kx, BF16), qx)


def _block_diag_rows(x, n_q):
    rows = N_HEADS * n_q
    tiled = jnp.concatenate([x] * N_HEADS, axis=0)
    r_h = lax.broadcasted_iota(jnp.int32, (rows, x.shape[1]), 0) // n_q
    c_h = lax.broadcasted_iota(jnp.int32, (rows, x.shape[1]), 1) // HEAD_DIM
    return jnp.where(r_h == c_h, tiled, 0.0)


def _pad_rows(x, rows):
    return jnp.concatenate([x, jnp.zeros((rows - x.shape[0], x.shape[1]), x.dtype)], axis=0)


def _head_major(page_refs):
    n_pages = len(page_refs) // HEAD_GROUPS
    cols = []
    for h in range(N_HEADS):
        hg, sl = divmod(h, SUBLANES)
        rows = [page_refs[i * HEAD_GROUPS + hg].reshape(PAGE_SIZE * SUBLANES, HEAD_DIM)[
            pl.ds(sl, PAGE_SIZE, stride=SUBLANES), :] for i in range(n_pages)]
        cols.append(jnp.concatenate(rows, axis=0).astype(BF16))
    return jnp.concatenate(cols, axis=1)


def _moba_decode_scores_kernel(pt_ref, q_ref, knew_ref, slope_ref, *rest, n_q, past_len):
    n_in = SCORE_PAGES_PER_STEP * HEAD_GROUPS
    pages = rest[:n_in]
    p_ref, pown_ref, l_ref, s_ref, ksum_ref, qbd_ref, pick_ref = rest[n_in:]
    c = pl.program_id(1)
    n_chunks = pl.num_programs(1)
    chunk = SCORE_PAGES_PER_STEP * PAGE_SIZE
    blocks_per_chunk = chunk // BLOCK
    n_blocks = past_len // BLOCK
    rows = N_HEADS * n_q

    @pl.when(c == 0)
    def _():
        qbd_ref[...] = _block_diag_rows(q_ref[0], n_q).T.astype(BF16)

    pages_per_block = BLOCK // PAGE_SIZE
    for bi in range(blocks_per_chunk):
        for hg in range(HEAD_GROUPS):
            tot = None
            for i in range(bi * pages_per_block, (bi + 1) * pages_per_block):
                part = jnp.sum(pages[i * HEAD_GROUPS + hg][...], axis=0)
                tot = part if tot is None else tot + part
            row = ((c * blocks_per_chunk + bi) * HEAD_GROUPS + hg) * SUBLANES
            ksum_ref[pl.ds(pl.multiple_of(row, SUBLANES), SUBLANES), :] = tot
    s_ref[c] = _dot(_head_major(pages), qbd_ref[...])

    @pl.when(c == n_chunks - 1)
    def _():
        qbd = qbd_ref[...]
        ksum = jnp.concatenate([ksum_ref[pl.ds(h, n_blocks, stride=N_HEADS), :]
                                for h in range(N_HEADS)], axis=1)
        kmean = (ksum / BLOCK).astype(BF16)
        gate = _dot(kmean, qbd)
        idx = lax.broadcasted_iota(jnp.int32, gate.shape, 0)
        picked = jnp.zeros(gate.shape, jnp.bool_)
        for _ in range(TOP_K):
            best = jnp.max(gate, axis=0, keepdims=True)
            first = jnp.min(jnp.where(gate == best, idx, n_blocks), axis=0, keepdims=True)
            hit = idx == first
            picked = picked | hit
            gate = jnp.where(hit, -jnp.inf, gate)
        pick_ref[...] = picked.astype(F32)

        slope = slope_ref[0:1, :]
        q_i = lax.rem(lax.broadcasted_iota(jnp.int32, (1, rows), 1), n_q)
        t = (past_len + q_i).astype(F32)

        def mask_chunk(ci, m_run):
            pos = (ci * chunk + lax.broadcasted_iota(jnp.int32, (chunk, 1), 0)).astype(F32)
            s = s_ref[ci] * SCALE - slope * (t - pos)
            parts = []
            for bi in range(blocks_per_chunk):
                keep = pick_ref[pl.ds(ci * blocks_per_chunk + bi, 1), :]
                parts.append(jnp.where(keep > 0.5, s[bi * BLOCK:(bi + 1) * BLOCK], -jnp.inf))
            s = jnp.concatenate(parts, axis=0)
            s_ref[ci] = s
            return jnp.maximum(m_run, jnp.max(s, axis=0, keepdims=True))

        m_past = lax.fori_loop(0, n_chunks, mask_chunk, jnp.full((1, rows), -jnp.inf, F32))

        k_own = _pad_rows(knew_ref[0], LANES).astype(BF16)
        j_own = lax.broadcasted_iota(jnp.int32, (LANES, rows), 0)
        s_own = _dot(k_own, qbd) * SCALE - slope * (q_i - j_own).astype(F32)
        s_own = jnp.where(j_own <= q_i, s_own, -jnp.inf)
        m_all = jnp.maximum(m_past, jnp.max(s_own, axis=0, keepdims=True))
        p_own = jnp.exp(s_own - m_all)
        pown_ref[0] = p_own.T

        def exp_chunk(ci, l_run):
            p = jnp.exp(s_ref[ci] - m_all)
            p_ref[0, ci] = p.T.astype(p_ref.dtype)
            return l_run + jnp.sum(p, axis=0, keepdims=True)

        l_all = lax.fori_loop(0, n_chunks, exp_chunk, jnp.sum(p_own, axis=0, keepdims=True))
        l_ref[0] = jnp.broadcast_to(l_all, (LANES, rows)).T


def _moba_decode_values_kernel(pt_ref, p_ref, pown_ref, l_ref, vnew_ref, g_ref, *rest, n_q):
    n_in = VALUE_PAGES_PER_STEP * HEAD_GROUPS
    pages = rest[:n_in]
    o_ref, acc_ref = rest[n_in:]
    c = pl.program_id(1)
    pages_per_block = BLOCK // PAGE_SIZE
    blocks_per_chunk = SCORE_PAGES_PER_STEP // pages_per_block

    part = None
    for blk in range(VALUE_PAGES_PER_STEP // pages_per_block):
        slabs = pages[blk * pages_per_block * HEAD_GROUPS:(blk + 1) * pages_per_block * HEAD_GROUPS]
        sub, within = divmod(blk, blocks_per_chunk)
        term = _dot(p_ref[0, sub, :, within * BLOCK:(within + 1) * BLOCK], _head_major(slabs))
        part = term if part is None else part + term

    @pl.when(c == 0)
    def _():
        acc_ref[...] = part

    @pl.when(c > 0)
    def _():
        acc_ref[...] += part

    @pl.when(c == pl.num_programs(1) - 1)
    def _():
        v_own = _pad_rows(vnew_ref[0], LANES).astype(BF16)
        full = (acc_ref[...] + _dot(pown_ref[0].astype(BF16), v_own)) / l_ref[0][:, 0:1]
        r_h = lax.broadcasted_iota(jnp.int32, full.shape, 0) // n_q
        c_h = lax.broadcasted_iota(jnp.int32, full.shape, 1) // HEAD_DIM
        o = jnp.sum(jnp.where(r_h == c_h, full, 0.0).reshape(N_HEADS, n_q, full.shape[1]), axis=0)
        gate = g_ref[0]
        o_ref[0] = o * (gate * jax.nn.sigmoid(gate))


def _moba_decode(q, g, k_new, v_new, cache_k, cache_v, page_table, n_batch, n_q):
    d = q.shape[1]
    n_pages = page_table.shape[1]
    past_len = n_pages * PAGE_SIZE
    assert n_q == SUBLANES and past_len % BLOCK == 0 and n_pages % VALUE_PAGES_PER_STEP == 0
    assert VALUE_PAGES_PER_STEP % SCORE_PAGES_PER_STEP == 0
    n_chunks = n_pages // SCORE_PAGES_PER_STEP
    chunk = SCORE_PAGES_PER_STEP * PAGE_SIZE
    chunks_per_value_step = VALUE_PAGES_PER_STEP // SCORE_PAGES_PER_STEP
    n_blocks = past_len // BLOCK
    rows = N_HEADS * n_q
    pt = page_table.reshape(-1).astype(jnp.int32)
    ck = cache_k.reshape(cache_k.shape[0] * PAGE_SIZE, N_HEADS, HEAD_DIM)
    cv = cache_v.reshape(cache_v.shape[0] * PAGE_SIZE, N_HEADS, HEAD_DIM)
    q3, g3 = q.reshape(n_batch, n_q, d), g.reshape(n_batch, n_q, d)
    kn3, vn3 = k_new.reshape(n_batch, n_q, d), v_new.reshape(n_batch, n_q, d)
    slope_cols = jnp.broadcast_to(jnp.repeat(_alibi_slopes(), n_q)[None, :], (SUBLANES, rows))

    def page_specs(pages_per_step):
        def one(i, hg):
            return pl.BlockSpec(
                (PAGE_SIZE, SUBLANES, HEAD_DIM),
                lambda b, c, pt_ref: (pt_ref[b * n_pages + c * pages_per_step + i], hg, 0))
        return [one(i, hg) for i in range(pages_per_step) for hg in range(HEAD_GROUPS)]

    tok_spec = pl.BlockSpec((1, n_q, d), lambda b, c, pt_ref: (b, 0, 0))
    row_spec = pl.BlockSpec((1, rows, LANES), lambda b, c, pt_ref: (b, 0, 0))
    k_specs = page_specs(SCORE_PAGES_PER_STEP)
    probs, p_own, denom = pl.pallas_call(
        functools.partial(_moba_decode_scores_kernel, n_q=n_q, past_len=past_len),
        out_shape=(jax.ShapeDtypeStruct((n_batch, n_chunks, rows, chunk), BF16),
                   jax.ShapeDtypeStruct((n_batch, rows, LANES), F32),
                   jax.ShapeDtypeStruct((n_batch, rows, LANES), F32)),
        grid_spec=pltpu.PrefetchScalarGridSpec(
            num_scalar_prefetch=1,
            grid=(n_batch, n_chunks),
            in_specs=[tok_spec, tok_spec,
                      pl.BlockSpec((SUBLANES, rows), lambda b, c, pt_ref: (0, 0))] + k_specs,
            out_specs=(pl.BlockSpec((1, n_chunks, rows, chunk), lambda b, c, pt_ref: (b, 0, 0, 0)),
                       row_spec, row_spec),
            scratch_shapes=[pltpu.VMEM((n_chunks, chunk, rows), F32),
                            pltpu.VMEM((n_blocks * N_HEADS, HEAD_DIM), F32),
                            pltpu.VMEM((d, rows), BF16),
                            pltpu.VMEM((n_blocks, rows), F32)]),
        compiler_params=_params("arbitrary", "arbitrary"),
        name="moba_decode_scores",
    )(pt, q3, kn3, slope_cols, *([ck] * len(k_specs)))

    v_specs = page_specs(VALUE_PAGES_PER_STEP)
    out = pl.pallas_call(
        functools.partial(_moba_decode_values_kernel, n_q=n_q),
        out_shape=jax.ShapeDtypeStruct((n_batch, n_q, d), F32),
        grid_spec=pltpu.PrefetchScalarGridSpec(
            num_scalar_prefetch=1,
            grid=(n_batch, n_pages // VALUE_PAGES_PER_STEP),
            in_specs=[pl.BlockSpec((1, chunks_per_value_step, rows, chunk),
                                   lambda b, c, pt_ref: (b, c, 0, 0)),
                      row_spec, row_spec, tok_spec, tok_spec] + v_specs,
            out_specs=tok_spec,
            scratch_shapes=[pltpu.VMEM((rows, d), F32)]),
        compiler_params=_params("arbitrary", "arbitrary"),
        name="moba_decode_values",
    )(pt, probs, p_own, denom, vn3, g3, *([cv] * len(v_specs)))
    return out.reshape(n_batch * n_q, d).astype(BF16)


def _cast_kernel(src_ref, dst_ref):
    dst_ref[...] = src_ref[...].astype(BF16)


def _cast_layer(w, layer, slab=128):
    _, r, c = w.shape
    return pl.pallas_call(
        _cast_kernel,
        out_shape=jax.ShapeDtypeStruct((1, r, c), BF16),
        grid=(r // slab,),
        in_specs=[pl.BlockSpec((None, slab, c), lambda i: (layer, i, 0))],
        out_specs=pl.BlockSpec((None, slab, c), lambda i: (0, i, 0)),
        compiler_params=_params("arbitrary"),
        name="cast_layer",
    )(w)


def _trunks(x_long, x_short, conv_prev_long, conv_prev_short, params, attend_long, attend_short):
    conv_w, ln_g, ln_b = params["conv_w"], params["ln_g"], params["ln_b"]
    n_l, t_l, d = x_long.shape
    n_s, t_s, _ = x_short.shape
    xl32, xs32 = x_long.reshape(n_l * t_l, d), x_short.reshape(n_s * t_s, d)
    xl16, xs16 = xl32.astype(BF16), xs32.astype(BF16)
    w16 = {("w_in_a", 0): _cast_layer(params["w_in_a"], 0)}

    def jobs(*names):
        return [(params[n] if params[n].ndim == 3 else params[n][None], l) for n, l in names]

    def keep(names, copies):
        w16.update(zip(names, copies))

    conv_l, conv_s = [], []
    ahead = [
        [("w_out_a", 0), ("w_in_a", 1), ("w_in_b", 1)],
        [("w_out_a", 1), ("w_kv", 0), ("w_in_b", 0), ("w_out_b", 0), ("w_out_b", 1)],
    ]
    for l in range(N_A):
        yl16, rows_l, copies, (ys16, rows_s) = _conv_proj(
            xl16, w16["w_in_a", l], 0, conv_w[l], conv_prev_long[l], t_l, jobs(*ahead[l]),
            short=(xs16, conv_prev_short[l], t_s))
        keep(ahead[l], copies)
        conv_l.append(rows_l)
        conv_s.append(rows_s)
        xl32, xl16, xs32, xs16 = _out_proj_ln(yl16, w16["w_out_a", l], 0, xl32, ln_g[l], ln_b[l],
                                              short=(ys16, xs32))
    kl, vl, _, (ks, vs) = _dual_proj(xl16, w16["w_kv", 0], 0, short=xs16)
    for l in range(N_A, DEPTH):
        ql, gl, _, (qs, gs) = _dual_proj(xl16, w16["w_in_b", l - N_A], 0, short=xs16)
        ol16 = attend_long(ql, gl, kl, vl)
        os16 = attend_short(qs, gs, ks, vs)
        xl32, xl16, xs32, xs16 = _out_proj_ln(ol16, w16["w_out_b", l - N_A], 0, xl32, ln_g[l], ln_b[l],
                                              short=(os16, xs32))
    heads_l, heads_s = (n_l, t_l, N_HEADS, HEAD_DIM), (n_s, t_s, N_HEADS, HEAD_DIM)
    return ((xl32.reshape(n_l, t_l, d), kl.reshape(heads_l), vl.reshape(heads_l), jnp.stack(conv_l)),
            (xs32.reshape(n_s, t_s, d), ks.reshape(heads_s), vs.reshape(heads_s), jnp.stack(conv_s)))


def kernel(x_prompt, x_sample, cache_k, cache_v, state_conv, page_table, w_in_a, conv_w, w_out_a,
           w_kv, w_in_b, w_out_b, ln_g, ln_b):
    params = dict(w_in_a=w_in_a, conv_w=conv_w, w_out_a=w_out_a, w_kv=w_kv, w_in_b=w_in_b,
                  w_out_b=w_out_b, ln_g=ln_g, ln_b=ln_b)
    n_prompt, seq, d = x_prompt.shape
    n_dec, dec_seq, _ = x_sample.shape
    zero_conv = jnp.zeros((N_A, n_prompt, CONV_W - 1, d), x_prompt.dtype)
    prompt, sample = _trunks(
        x_prompt, x_sample, zero_conv, state_conv, params,
        lambda q, g, k, v: _moba_prefill(q, k, v, g, n_prompt, seq),
        lambda q, g, k, v: _moba_decode(q, g, k, v, cache_k, cache_v, page_table, n_dec, dec_seq))
    y_prompt, k_prompt, v_prompt, conv_prompt = prompt
    y_sample, k_sample, v_sample, conv_sample = sample
    return (y_prompt, y_sample, k_prompt, v_prompt, conv_prompt, k_sample, v_sample, conv_sample)
```

```python
import functools
import math

import numpy as np
import jax
import jax.numpy as jnp
from jax import lax
from jax.experimental import pallas as pl
from jax.experimental.pallas import tpu as pltpu

N_HEADS = 32
HEAD_DIM = 128
CONV_W = 3
BLOCK = 256
TOP_K = 3
PAGE_SIZE = 128
DEPTH = 4
N_A = DEPTH // 2
ALPHA = (2.0 * DEPTH) ** 0.25
LN_EPS = 1e-5
SCALE = 1.0 / math.sqrt(HEAD_DIM)
LOG2_E = math.log2(math.e)

SUBLANES = 8
LANES = 128
VMEM_LIMIT_BYTES = 56 * 1024 * 1024
LN_VMEM_LIMIT_BYTES = 58 * 1024 * 1024
MASK_BIG = 2.0 ** 100
SCORE_PAGES_PER_STEP = 8
VALUE_PAGES_PER_STEP = 8
HEAD_GROUPS = N_HEADS // SUBLANES
ALIBI_PARTS = 4
HEADS_PER_STEP = 2

F32 = jnp.float32
BF16 = jnp.bfloat16


def _params(*semantics, vmem_limit=VMEM_LIMIT_BYTES):
    return pltpu.CompilerParams(dimension_semantics=semantics, vmem_limit_bytes=vmem_limit)


def _dot(a, b):
    return jnp.dot(a, b, preferred_element_type=F32)


def _dot_nt(a, b):
    return lax.dot_general(a, b, (((1,), (1,)), ((), ())), preferred_element_type=F32)


def _cast_jobs(jobs, n_steps, step_of):
    in_specs, out_specs, out_shapes, operands = [], [], [], []
    for w, layer in jobs:
        _, r, c = w.shape
        assert r % n_steps == 0 and (r // n_steps) % 16 == 0
        slab = r // n_steps
        in_specs.append(pl.BlockSpec(
            (None, slab, c), functools.partial(lambda *ids, layer: (layer, step_of(*ids), 0), layer=layer)))
        out_specs.append(pl.BlockSpec((None, slab, c), lambda *ids: (0, step_of(*ids), 0)))
        out_shapes.append(jax.ShapeDtypeStruct((1, r, c), BF16))
        operands.append(w)
    return in_specs, out_specs, out_shapes, operands


def _run_casts(src_refs, dst_refs):
    for src, dst in zip(src_refs, dst_refs):
        dst[...] = src[...].astype(BF16)


def _gated_conv(b_g, u, u_m1, u_m2, z, cw_ref):
    conv = cw_ref[0:1, :] * u_m2 + cw_ref[1:2, :] * u_m1 + cw_ref[2:3, :] * u
    return (b_g * conv) * (z * jax.nn.sigmoid(z))


def _conv_rows(x, w_refs, cw_ref, u_prev):
    wb_ref, wc_ref, wh_ref, wz_ref = w_refs
    u = _dot(x, wc_ref[...]) * _dot(x, wh_ref[...])
    u_m1, u_m2 = u_prev(u)
    return _gated_conv(_dot(x, wb_ref[...]), u, u_m1, u_m2, _dot(x, wz_ref[...]), cw_ref), u


def _conv_proj_kernel(x_ref, wb_ref, wc_ref, wh_ref, wz_ref, cw_ref, prev_ref, *rest,
                      tiles_per_seq, n_casts, short_len):
    n_short = 3 if short_len else 0
    short_in = rest[:n_short]
    cast_src = rest[n_short:n_short + n_casts]
    outs = rest[n_short + n_casts:]
    y_ref, tail_ref = outs[:2]
    short_out = outs[2:2 + (2 if short_len else 0)]
    cast_dst = outs[2 + len(short_out):2 + len(short_out) + n_casts]
    carry_ref = outs[-1]
    w_refs = (wb_ref, wc_ref, wh_ref, wz_ref)
    _run_casts(cast_src, cast_dst)
    m = pl.program_id(1)

    @pl.when(lax.rem(m, tiles_per_seq) == 0)
    def _():
        carry_ref[...] = prev_ref[0]

    def carried(u):
        row = lax.broadcasted_iota(jnp.int32, u.shape, 0)
        c_m2 = carry_ref[SUBLANES - 2:SUBLANES - 1, :]
        c_m1 = carry_ref[SUBLANES - 1:SUBLANES, :]
        u_m1 = jnp.where(row == 0, c_m1, pltpu.roll(u, 1, 0))
        u_m2 = jnp.where(row == 0, c_m2, jnp.where(row == 1, c_m1, pltpu.roll(u, 2, 0)))
        return u_m1, u_m2

    y, u = _conv_rows(x_ref[...], w_refs, cw_ref, carried)
    y_ref[...] = y.astype(y_ref.dtype)
    tail = u[u.shape[0] - SUBLANES:, :]
    carry_ref[...] = tail
    tail_ref[0] = tail

    if short_len:
        xs_ref, pm1_ref, pm2_ref = short_in
        ys_ref, us_ref = short_out

        @pl.when(m == 0)
        def _():
            def preceded(u_s):
                pos = lax.rem(lax.broadcasted_iota(jnp.int32, u_s.shape, 0), short_len)
                return (jnp.where(pos == 0, pm1_ref[...], pltpu.roll(u_s, 1, 0)),
                        jnp.where(pos < 2, pm2_ref[...], pltpu.roll(u_s, 2, 0)))

            y_s, u_s = _conv_rows(xs_ref[...], w_refs, cw_ref, preceded)
            ys_ref[...] = y_s.astype(ys_ref.dtype)
            us_ref[...] = u_s


def _conv_proj(x16, w_in16, layer, conv_w, conv_prev, seq_len, casts=(), short=None):
    m_rows, d = x16.shape
    n_seq = m_rows // seq_len
    tn = 256
    nj = d // tn
    w_specs = [pl.BlockSpec((None, d, tn),
                            functools.partial(lambda j, m, g: (layer, 0, g * nj + j), g=g))
               for g in range(4)]
    tm = 1024 if seq_len % 1024 == 0 else 512
    assert seq_len % tm == 0
    tiles_per_seq = seq_len // tm
    nm = m_rows // tm
    prev8 = jnp.pad(conv_prev.astype(F32), ((0, 0), (SUBLANES - (CONV_W - 1), 0), (0, 0)))
    c_in, c_out, c_shapes, c_ops = _cast_jobs(casts, nj * nm, lambda j, m: j * nm + m)
    s_in, s_out, s_shapes, s_ops, short_len = [], [], [], [], 0
    if short is not None:
        xs16, prev_s, short_len = short
        ms = xs16.shape[0]
        n_s = ms // short_len
        assert short_len >= CONV_W - 1 and ms % 16 == 0
        prev_s = prev_s.astype(F32)
        zeros = jnp.zeros((n_s, short_len - 1, d), F32)
        pm1 = jnp.concatenate([prev_s[:, 1:2], zeros], axis=1).reshape(ms, d)
        pm2 = jnp.concatenate([prev_s, zeros[:, 1:]], axis=1).reshape(ms, d)
        col = pl.BlockSpec((ms, tn), lambda j, m: (0, j))
        s_in = [pl.BlockSpec((ms, d), lambda j, m: (0, 0)), col, col]
        s_out = [col, col]
        s_shapes = [jax.ShapeDtypeStruct((ms, d), BF16), jax.ShapeDtypeStruct((ms, d), F32)]
        s_ops = [xs16, pm1, pm2]
    outs = pl.pallas_call(
        functools.partial(_conv_proj_kernel, tiles_per_seq=tiles_per_seq, n_casts=len(casts),
                          short_len=short_len),
        out_shape=(jax.ShapeDtypeStruct((m_rows, d), BF16),
                   jax.ShapeDtypeStruct((n_seq, SUBLANES, d), F32), *s_shapes, *c_shapes),
        grid=(nj, nm),
        in_specs=[pl.BlockSpec((tm, d), lambda j, m: (m, 0))] + w_specs + [
            pl.BlockSpec((CONV_W, tn), lambda j, m: (0, j)),
            pl.BlockSpec((1, SUBLANES, tn), lambda j, m: (m // tiles_per_seq, 0, j))] + s_in + c_in,
        out_specs=(pl.BlockSpec((tm, tn), lambda j, m: (m, j)),
                   pl.BlockSpec((1, SUBLANES, tn), lambda j, m: (m // tiles_per_seq, 0, j)),
                   *s_out, *c_out),
        scratch_shapes=[pltpu.VMEM((SUBLANES, tn), F32)],
        compiler_params=_params("arbitrary", "arbitrary"),
        name="conv_proj",
    )(x16, w_in16, w_in16, w_in16, w_in16, conv_w, prev8, *s_ops, *c_ops)
    y, tail = outs[:2]
    short_res = None
    if short is not None:
        y_s, u_s = outs[2:4]
        short_res = (y_s, u_s.reshape(n_s, short_len, d)[:, short_len - (CONV_W - 1):, :])
    return y, tail[:, SUBLANES - (CONV_W - 1):, :], list(outs[2 + len(s_shapes):]), short_res


def _fold_lanes(v):
    out = v[:, :LANES]
    for t in range(1, v.shape[1] // LANES):
        out = out + v[:, t * LANES:(t + 1) * LANES]
    return out


def _ln_collect(r, j, nj, d, r_ref, shift_ref, sum_ref, sq_ref, mean_ref, rstd_ref):
    r_ref[j] = r

    @pl.when(j == 0)
    def _():
        shift_ref[...] = jnp.broadcast_to(jnp.mean(r, axis=-1, keepdims=True), shift_ref.shape)
        sum_ref[...] = jnp.zeros_like(sum_ref)
        sq_ref[...] = jnp.zeros_like(sq_ref)

    dlt = r - shift_ref[:, 0:1]
    sum_ref[...] += _fold_lanes(dlt)
    sq_ref[...] += _fold_lanes(dlt * dlt)

    @pl.when(j == nj - 1)
    def _():
        e1 = jnp.sum(sum_ref[...], axis=-1, keepdims=True) / d
        e2 = jnp.sum(sq_ref[...], axis=-1, keepdims=True) / d
        var = jnp.maximum(e2 - e1 * e1, 0.0)
        mean_ref[...] = jnp.broadcast_to(shift_ref[:, 0:1] + e1, mean_ref.shape)
        rstd_ref[...] = jnp.broadcast_to(lax.rsqrt(var + LN_EPS), rstd_ref.shape)


def _ln_emit(j, r_ref, mean_ref, rstd_ref, g_ref, b_ref, xo_ref, xo16_ref):
    o = (r_ref[j] - mean_ref[:, 0:1]) * rstd_ref[:, 0:1] * g_ref[...] + b_ref[...]
    xo_ref[...] = o
    xo16_ref[...] = o.astype(xo16_ref.dtype)


def _out_proj_ln_kernel(y_ref, w_ref, xres_ref, g_ref, b_ref, *rest, nm, nj, d, has_short):
    n_s = 2 if has_short else 0
    short_in = rest[:n_s]
    xo_ref, xo16_ref = rest[n_s:n_s + 2]
    short_out = rest[n_s + 2:2 * n_s + 2]
    scratch = rest[2 * n_s + 2:]
    main_state, short_state = scratch[:6], scratch[6:]
    mm = pl.program_id(0)
    j = pl.program_id(1)

    def emit():
        _ln_emit(j, main_state[0], main_state[4], main_state[5], g_ref, b_ref, xo_ref, xo16_ref)

    def build(with_short):
        y = y_ref[...]
        tm = y.shape[0]
        if with_short:
            ys_ref, xres_s_ref = short_in
            y = jnp.concatenate([y, ys_ref[...]], axis=0)
        prod = _dot(y, w_ref[...])
        _ln_collect(ALPHA * xres_ref[...] + prod[:tm], j, nj, d, *main_state)
        if with_short:
            _ln_collect(ALPHA * xres_s_ref[...] + prod[tm:], j, nj, d, *short_state)

    def emit_and_build():
        emit()
        build(False)

    if has_short:
        @pl.when(mm == 1)
        def _():
            _ln_emit(j, short_state[0], short_state[4], short_state[5], g_ref, b_ref, *short_out)

    pl.when(mm == 0)(functools.partial(build, has_short))
    pl.when((mm > 0) & (mm < nm))(emit_and_build)
    pl.when(mm == nm)(emit)


def _out_proj_ln(y16, w16, layer, x_res, gain, bias, short=None):
    m_rows, k = y16.shape
    d = w16.shape[2]
    tm = min(1024, m_rows)
    tn = 512
    assert m_rows % tm == 0 and d % tn == 0
    nj = d // tn
    nm = m_rows // tm
    row_in = lambda mm: jnp.minimum(mm, nm - 1)
    col_in = lambda mm, j: jnp.where(mm < nm, j, nj - 1)
    out_idx = lambda mm, j: (jnp.maximum(mm - 1, 0), jnp.where(mm == 0, 0, j))

    def state(rows):
        stat = pltpu.VMEM((rows, LANES), F32)
        return [pltpu.VMEM((nj, rows, tn), F32), stat, stat, stat, stat, stat]

    s_in, s_out, s_shapes, s_ops, s_state = [], [], [], [], []
    if short is not None:
        ys16, x_res_s = short
        ms = ys16.shape[0]
        s_in = [pl.BlockSpec((ms, k), lambda mm, j: (0, 0)),
                pl.BlockSpec((ms, tn), lambda mm, j: (0, jnp.where(mm == 0, j, nj - 1)))]
        s_idx = lambda mm, j: (0, jnp.where(mm == 0, 0, jnp.where(mm == 1, j, nj - 1)))
        s_out = [pl.BlockSpec((ms, tn), s_idx), pl.BlockSpec((ms, tn), s_idx)]
        s_shapes = [jax.ShapeDtypeStruct((ms, d), F32), jax.ShapeDtypeStruct((ms, d), BF16)]
        s_ops = [ys16, x_res_s]
        s_state = state(ms)
        assert nm >= 1
    return pl.pallas_call(
        functools.partial(_out_proj_ln_kernel, nm=nm, nj=nj, d=d, has_short=short is not None),
        out_shape=(jax.ShapeDtypeStruct((m_rows, d), F32), jax.ShapeDtypeStruct((m_rows, d), BF16),
                   *s_shapes),
        grid=(nm + 1, nj),
        in_specs=[pl.BlockSpec((tm, k), lambda mm, j: (row_in(mm), 0)),
                  pl.BlockSpec((None, k, tn), lambda mm, j: (layer, 0, col_in(mm, j))),
                  pl.BlockSpec((tm, tn), lambda mm, j: (row_in(mm), col_in(mm, j))),
                  pl.BlockSpec((1, tn), lambda mm, j: (0, j)),
                  pl.BlockSpec((1, tn), lambda mm, j: (0, j))] + s_in,
        out_specs=(pl.BlockSpec((tm, tn), out_idx), pl.BlockSpec((tm, tn), out_idx), *s_out),
        scratch_shapes=state(tm) + s_state,
        compiler_params=_params("arbitrary", "arbitrary", vmem_limit=LN_VMEM_LIMIT_BYTES),
        name="out_proj_ln",
    )(y16, w16, x_res, gain.reshape(1, d), bias.reshape(1, d), *s_ops)


def _dual_proj_kernel(x_ref, wa_ref, wb_ref, *rest, n_casts, has_short):
    n_s = 1 if has_short else 0
    cast_src = rest[n_s:n_s + n_casts]
    outs = rest[n_s + n_casts:]
    oa_ref, ob_ref = outs[:2]
    _run_casts(cast_src, outs[2 + 2 * n_s:])

    def run(with_short):
        x = x_ref[...]
        tm = x.shape[0]
        if with_short:
            x = jnp.concatenate([x, rest[0][...]], axis=0)
        prod_a = _dot(x, wa_ref[...])
        prod_b = _dot(x, wb_ref[...])
        oa_ref[...] = prod_a[:tm]
        ob_ref[...] = prod_b[:tm]
        if with_short:
            oas_ref, obs_ref = outs[2:4]
            oas_ref[...] = prod_a[tm:]
            obs_ref[...] = prod_b[tm:]

    if has_short:
        first = pl.program_id(0) == 0
        pl.when(first)(functools.partial(run, True))
        pl.when(jnp.logical_not(first))(functools.partial(run, False))
    else:
        run(False)


def _dual_proj(x16, w16, layer, casts=(), short=None):
    m_rows, k = x16.shape
    n_half = w16.shape[2] // 2
    tm = next(t for t in (1024, 512, m_rows) if m_rows % t == 0)
    tn = 512
    nj = n_half // tn
    out = jax.ShapeDtypeStruct((m_rows, n_half), F32)
    nm = m_rows // tm
    c_in, c_out, c_shapes, c_ops = _cast_jobs(casts, nm * nj, lambda m, j: m * nj + j)
    s_in, s_out, s_shapes, s_ops = [], [], [], []
    if short is not None:
        ms = short.shape[0]
        s_idx = lambda m, j: (0, jnp.where(m == 0, j, nj - 1))
        s_in = [pl.BlockSpec((ms, k), lambda m, j: (0, 0))]
        s_out = [pl.BlockSpec((ms, tn), s_idx), pl.BlockSpec((ms, tn), s_idx)]
        s_shapes = [jax.ShapeDtypeStruct((ms, n_half), F32)] * 2
        s_ops = [short]
    outs = pl.pallas_call(
        functools.partial(_dual_proj_kernel, n_casts=len(casts), has_short=short is not None),
        out_shape=(out, out, *s_shapes, *c_shapes),
        grid=(nm, nj),
        in_specs=[pl.BlockSpec((tm, k), lambda m, j: (m, 0)),
                  pl.BlockSpec((None, k, tn), lambda m, j: (layer, 0, j)),
                  pl.BlockSpec((None, k, tn), lambda m, j: (layer, 0, nj + j))] + s_in + c_in,
        out_specs=(pl.BlockSpec((tm, tn), lambda m, j: (m, j)),
                   pl.BlockSpec((tm, tn), lambda m, j: (m, j)), *s_out, *c_out),
        compiler_params=_params("arbitrary", "arbitrary"),
        name="dual_proj",
    )(x16, w16, w16, *s_ops, *c_ops)
    n_short = len(s_shapes)
    return outs[0], outs[1], list(outs[2 + n_short:]), tuple(outs[2:2 + n_short])


def _moba_prefill_kernel(q_ref, k_ref, v_ref, g_ref, kx_ref, qx_ref, o_ref, *scratch):
    seq = q_ref.shape[0]
    nb = seq // BLOCK
    per_head = 3 + nb
    r_i = lax.broadcasted_iota(jnp.int32, (BLOCK, BLOCK), 0)
    c_i = lax.broadcasted_iota(jnp.int32, (BLOCK, BLOCK), 1)
    causal = jnp.where(c_i <= r_i, 0.0, -jnp.inf)
    state = []
    for hh in range(HEADS_PER_STEP):
        qa_ref, ka_ref, vb_ref = scratch[hh * per_head:hh * per_head + 3]
        s_refs = scratch[hh * per_head + 3:(hh + 1) * per_head]
        cols = slice(hh * HEAD_DIM, (hh + 1) * HEAD_DIM)
        q16 = q_ref[:, cols].astype(BF16)
        k = k_ref[:, cols]
        kmean = jnp.mean(k.reshape(nb, BLOCK, HEAD_DIM), axis=1)
        gate_t = _dot_nt(kmean.astype(BF16), q16)
        blk = lax.broadcasted_iota(jnp.int32, (nb, seq), 0)
        q_blk = lax.broadcasted_iota(jnp.int32, (nb, seq), 1) // BLOCK
        rank = jnp.zeros((nb, seq), jnp.int32)
        for other in range(nb):
            g_o = gate_t[other:other + 1, :]
            beats = (g_o > gate_t) | ((g_o == gate_t) & (other < blk))
            rank += jnp.where((q_blk > other) & beats, 1, 0)
        dropped = ((blk < q_blk) & (rank >= TOP_K)).astype(F32)
        lane = lax.broadcasted_iota(jnp.int32, (nb, LANES), 1)
        spread = jnp.where(lane == lax.broadcasted_iota(jnp.int32, (nb, LANES), 0), -MASK_BIG, 0.0)
        sel_bias = lax.dot_general(dropped, spread, (((0,), (0,)), ((), ())),
                                   preferred_element_type=F32)
        qa_ref[:, :HEAD_DIM] = q16
        qa_ref[:, HEAD_DIM:] = (sel_bias + qx_ref[hh, 0:1, :]).astype(BF16)
        ka_ref[:, :HEAD_DIM] = k.astype(BF16)
        ka_ref[:, HEAD_DIM:] = kx_ref[...]
        vb_ref[:, :HEAD_DIM] = v_ref[:, cols].astype(BF16)
        ones_col = lax.broadcasted_iota(jnp.int32, (seq, HEAD_DIM), 1) == 0
        vb_ref[:, HEAD_DIM:] = jnp.where(ones_col, 1.0, 0.0).astype(BF16)
        state.append((qa_ref, ka_ref, vb_ref, s_refs, cols))
    for i in range(nb):
        for qa_ref, ka_ref, vb_ref, s_refs, cols in state:
            s_refs[i][...] = _dot_nt(qa_ref[i * BLOCK:, :], ka_ref[i * BLOCK:(i + 1) * BLOCK, :])
        for qa_ref, ka_ref, vb_ref, s_refs, cols in state:
            rows = slice(i * BLOCK, (i + 1) * BLOCK)
            n_keys = (i + 1) * BLOCK
            parts = [s_refs[n][(i - n) * BLOCK:(i - n + 1) * BLOCK, :] for n in range(i)]
            parts.append(s_refs[i][0:BLOCK, :] + causal)
            s = parts[0] if i == 0 else jnp.concatenate(parts, axis=1)
            m = jnp.max(s, axis=-1, keepdims=True)
            p = jnp.exp2((s - m) * (SCALE * LOG2_E))
            ol = _dot(p.astype(BF16), vb_ref[0:n_keys, :])
            o = ol[:, :HEAD_DIM] / ol[:, HEAD_DIM:HEAD_DIM + 1]
            gate = g_ref[rows, cols]
            o_ref[rows, cols] = (o * (gate * jax.nn.sigmoid(gate))).astype(o_ref.dtype)


def _alibi_slopes():
    return 2.0 ** (-8.0 * jnp.arange(1, N_HEADS + 1, dtype=F32) / N_HEADS)


def _moba_prefill(q, k, v, g, n_batch, seq):
    nb = seq // BLOCK
    assert nb <= SUBLANES and seq % BLOCK == 0
    pos = np.arange(seq)
    kx = np.zeros((seq, LANES), np.float32)
    kx[pos, pos // BLOCK] = 1.0
    kx[:, SUBLANES:SUBLANES + ALIBI_PARTS] = (pos % BLOCK)[:, None]
    kx[:, SUBLANES + ALIBI_PARTS:SUBLANES + 2 * ALIBI_PARTS] = ((pos // BLOCK) * BLOCK)[:, None]
    rest = _alibi_slopes() / SCALE
    pieces = []
    for _ in range(ALIBI_PARTS):
        piece = rest.astype(BF16).astype(F32)
        pieces.append(piece)
        rest = rest - piece
    pieces = jnp.stack(pieces, axis=1)
    qx = jnp.zeros((N_HEADS, LANES), F32)
    qx = qx.at[:, SUBLANES:SUBLANES + ALIBI_PARTS].set(pieces)
    qx = qx.at[:, SUBLANES + ALIBI_PARTS:SUBLANES + 2 * ALIBI_PARTS].set(pieces)
    qx = jnp.broadcast_to(qx[:, None, :], (N_HEADS, SUBLANES, LANES))
    pair_spec = pl.BlockSpec((seq, HEADS_PER_STEP * HEAD_DIM), lambda b, h: (b, h))
    wide = pltpu.VMEM((seq, 2 * HEAD_DIM), BF16)
    per_head = [wide, wide, wide] + [pltpu.VMEM(((nb - i) * BLOCK, BLOCK), F32) for i in range(nb)]
    return pl.pallas_call(
        _moba_prefill_kernel,
        out_shape=jax.ShapeDtypeStruct(q.shape, BF16),
        grid=(n_batch, N_HEADS // HEADS_PER_STEP),
        in_specs=[pair_spec, pair_spec, pair_spec, pair_spec,
                  pl.BlockSpec((seq, LANES), lambda b, h: (0, 0)),
                  pl.BlockSpec((HEADS_PER_STEP, SUBLANES, LANES), lambda b, h: (h, 0, 0))],
        out_specs=pair_spec,
        scratch_shapes=per_head * HEADS_PER_STEP,
        compiler_params=_params("arbitrary", "arbitrary"),
        name="moba_prefill",
    )(q, k, v, g, jnp.asarray(kx, BF16), qx)


def _block_diag_rows(x, n_q):
    rows = N_HEADS * n_q
    tiled = jnp.concatenate([x] * N_HEADS, axis=0)
    r_h = lax.broadcasted_iota(jnp.int32, (rows, x.shape[1]), 0) // n_q
    c_h = lax.broadcasted_iota(jnp.int32, (rows, x.shape[1]), 1) // HEAD_DIM
    return jnp.where(r_h == c_h, tiled, 0.0)


def _pad_rows(x, rows):
    return jnp.concatenate([x, jnp.zeros((rows - x.shape[0], x.shape[1]), x.dtype)], axis=0)


def _head_major(page_refs):
    n_pages = len(page_refs) // HEAD_GROUPS
    cols = []
    for h in range(N_HEADS):
        hg, sl = divmod(h, SUBLANES)
        rows = [page_refs[i * HEAD_GROUPS + hg].reshape(PAGE_SIZE * SUBLANES, HEAD_DIM)[
            pl.ds(sl, PAGE_SIZE, stride=SUBLANES), :] for i in range(n_pages)]
        cols.append(jnp.concatenate(rows, axis=0).astype(BF16))
    return jnp.concatenate(cols, axis=1)


def _moba_decode_scores_kernel(pt_ref, q_ref, knew_ref, slope_ref, *rest, n_q, past_len):
    n_in = SCORE_PAGES_PER_STEP * HEAD_GROUPS
    pages = rest[:n_in]
    p_ref, pown_ref, l_ref, s_ref, ksum_ref, qbd_ref, pick_ref = rest[n_in:]
    c = pl.program_id(1)
    n_chunks = pl.num_programs(1)
    chunk = SCORE_PAGES_PER_STEP * PAGE_SIZE
    blocks_per_chunk = chunk // BLOCK
    n_blocks = past_len // BLOCK
    rows = N_HEADS * n_q

    @pl.when(c == 0)
    def _():
        qbd_ref[...] = _block_diag_rows(q_ref[0], n_q).T.astype(BF16)

    pages_per_block = BLOCK // PAGE_SIZE
    for bi in range(blocks_per_chunk):
        for hg in range(HEAD_GROUPS):
            tot = None
            for i in range(bi * pages_per_block, (bi + 1) * pages_per_block):
                part = jnp.sum(pages[i * HEAD_GROUPS + hg][...], axis=0)
                tot = part if tot is None else tot + part
            row = ((c * blocks_per_chunk + bi) * HEAD_GROUPS + hg) * SUBLANES
            ksum_ref[pl.ds(pl.multiple_of(row, SUBLANES), SUBLANES), :] = tot
    s_ref[c] = _dot(_head_major(pages), qbd_ref[...])

    @pl.when(c == n_chunks - 1)
    def _():
        qbd = qbd_ref[...]
        ksum = jnp.concatenate([ksum_ref[pl.ds(h, n_blocks, stride=N_HEADS), :]
                                for h in range(N_HEADS)], axis=1)
        kmean = (ksum / BLOCK).astype(BF16)
        gate = _dot(kmean, qbd)
        idx = lax.broadcasted_iota(jnp.int32, gate.shape, 0)
        picked = jnp.zeros(gate.shape, jnp.bool_)
        for _ in range(TOP_K):
            best = jnp.max(gate, axis=0, keepdims=True)
            first = jnp.min(jnp.where(gate == best, idx, n_blocks), axis=0, keepdims=True)
            hit = idx == first
            picked = picked | hit
            gate = jnp.where(hit, -jnp.inf, gate)
        pick_ref[...] = picked.astype(F32)

        slope = slope_ref[0:1, :]
        q_i = lax.rem(lax.broadcasted_iota(jnp.int32, (1, rows), 1), n_q)
        t = (past_len + q_i).astype(F32)

        def mask_chunk(ci, m_run):
            pos = (ci * chunk + lax.broadcasted_iota(jnp.int32, (chunk, 1), 0)).astype(F32)
            s = s_ref[ci] * SCALE - slope * (t - pos)
            parts = []
            for bi in range(blocks_per_chunk):
                keep = pick_ref[pl.ds(ci * blocks_per_chunk + bi, 1), :]
                parts.append(jnp.where(keep > 0.5, s[bi * BLOCK:(bi + 1) * BLOCK], -jnp.inf))
            s = jnp.concatenate(parts, axis=0)
            s_ref[ci] = s
            return jnp.maximum(m_run, jnp.max(s, axis=0, keepdims=True))

        m_past = lax.fori_loop(0, n_chunks, mask_chunk, jnp.full((1, rows), -jnp.inf, F32))

        k_own = _pad_rows(knew_ref[0], LANES).astype(BF16)
        j_own = lax.broadcasted_iota(jnp.int32, (LANES, rows), 0)
        s_own = _dot(k_own, qbd) * SCALE - slope * (q_i - j_own).astype(F32)
        s_own = jnp.where(j_own <= q_i, s_own, -jnp.inf)
        m_all = jnp.maximum(m_past, jnp.max(s_own, axis=0, keepdims=True))
        p_own = jnp.exp(s_own - m_all)
        pown_ref[0] = p_own.T

        def exp_chunk(ci, l_run):
            p = jnp.exp(s_ref[ci] - m_all)
            p_ref[0, ci] = p.T.astype(p_ref.dtype)
            return l_run + jnp.sum(p, axis=0, keepdims=True)

        l_all = lax.fori_loop(0, n_chunks, exp_chunk, jnp.sum(p_own, axis=0, keepdims=True))
        l_ref[0] = jnp.broadcast_to(l_all, (LANES, rows)).T


def _moba_decode_values_kernel(pt_ref, p_ref, pown_ref, l_ref, vnew_ref, g_ref, *rest, n_q):
    n_in = VALUE_PAGES_PER_STEP * HEAD_GROUPS
    pages = rest[:n_in]
    o_ref, acc_ref = rest[n_in:]
    c = pl.program_id(1)
    pages_per_block = BLOCK // PAGE_SIZE
    blocks_per_chunk = SCORE_PAGES_PER_STEP // pages_per_block

    part = None
    for blk in range(VALUE_PAGES_PER_STEP // pages_per_block):
        slabs = pages[blk * pages_per_block * HEAD_GROUPS:(blk + 1) * pages_per_block * HEAD_GROUPS]
        sub, within = divmod(blk, blocks_per_chunk)
        term = _dot(p_ref[0, sub, :, within * BLOCK:(within + 1) * BLOCK], _head_major(slabs))
        part = term if part is None else part + term

    @pl.when(c == 0)
    def _():
        acc_ref[...] = part

    @pl.when(c > 0)
    def _():
        acc_ref[...] += part

    @pl.when(c == pl.num_programs(1) - 1)
    def _():
        v_own = _pad_rows(vnew_ref[0], LANES).astype(BF16)
        full = (acc_ref[...] + _dot(pown_ref[0].astype(BF16), v_own)) / l_ref[0][:, 0:1]
        r_h = lax.broadcasted_iota(jnp.int32, full.shape, 0) // n_q
        c_h = lax.broadcasted_iota(jnp.int32, full.shape, 1) // HEAD_DIM
        o = jnp.sum(jnp.where(r_h == c_h, full, 0.0).reshape(N_HEADS, n_q, full.shape[1]), axis=0)
        gate = g_ref[0]
        o_ref[0] = o * (gate * jax.nn.sigmoid(gate))


def _moba_decode(q, g, k_new, v_new, cache_k, cache_v, page_table, n_batch, n_q):
    d = q.shape[1]
    n_pages = page_table.shape[1]
    past_len = n_pages * PAGE_SIZE
    assert n_q == SUBLANES and past_len % BLOCK == 0 and n_pages % VALUE_PAGES_PER_STEP == 0
    assert VALUE_PAGES_PER_STEP % SCORE_PAGES_PER_STEP == 0
    n_chunks = n_pages // SCORE_PAGES_PER_STEP
    chunk = SCORE_PAGES_PER_STEP * PAGE_SIZE
    chunks_per_value_step = VALUE_PAGES_PER_STEP // SCORE_PAGES_PER_STEP
    n_blocks = past_len // BLOCK
    rows = N_HEADS * n_q
    pt = page_table.reshape(-1).astype(jnp.int32)
    ck = cache_k.reshape(cache_k.shape[0] * PAGE_SIZE, N_HEADS, HEAD_DIM)
    cv = cache_v.reshape(cache_v.shape[0] * PAGE_SIZE, N_HEADS, HEAD_DIM)
    q3, g3 = q.reshape(n_batch, n_q, d), g.reshape(n_batch, n_q, d)
    kn3, vn3 = k_new.reshape(n_batch, n_q, d), v_new.reshape(n_batch, n_q, d)
    slope_cols = jnp.broadcast_to(jnp.repeat(_alibi_slopes(), n_q)[None, :], (SUBLANES, rows))

    def page_specs(pages_per_step):
        def one(i, hg):
            return pl.BlockSpec(
                (PAGE_SIZE, SUBLANES, HEAD_DIM),
                lambda b, c, pt_ref: (pt_ref[b * n_pages + c * pages_per_step + i], hg, 0))
        return [one(i, hg) for i in range(pages_per_step) for hg in range(HEAD_GROUPS)]

    tok_spec = pl.BlockSpec((1, n_q, d), lambda b, c, pt_ref: (b, 0, 0))
    row_spec = pl.BlockSpec((1, rows, LANES), lambda b, c, pt_ref: (b, 0, 0))
    k_specs = page_specs(SCORE_PAGES_PER_STEP)
    probs, p_own, denom = pl.pallas_call(
        functools.partial(_moba_decode_scores_kernel, n_q=n_q, past_len=past_len),
        out_shape=(jax.ShapeDtypeStruct((n_batch, n_chunks, rows, chunk), BF16),
                   jax.ShapeDtypeStruct((n_batch, rows, LANES), F32),
                   jax.ShapeDtypeStruct((n_batch, rows, LANES), F32)),
        grid_spec=pltpu.PrefetchScalarGridSpec(
            num_scalar_prefetch=1,
            grid=(n_batch, n_chunks),
            in_specs=[tok_spec, tok_spec,
                      pl.BlockSpec((SUBLANES, rows), lambda b, c, pt_ref: (0, 0))] + k_specs,
            out_specs=(pl.BlockSpec((1, n_chunks, rows, chunk), lambda b, c, pt_ref: (b, 0, 0, 0)),
                       row_spec, row_spec),
            scratch_shapes=[pltpu.VMEM((n_chunks, chunk, rows), F32),
                            pltpu.VMEM((n_blocks * N_HEADS, HEAD_DIM), F32),
                            pltpu.VMEM((d, rows), BF16),
                            pltpu.VMEM((n_blocks, rows), F32)]),
        compiler_params=_params("arbitrary", "arbitrary"),
        name="moba_decode_scores",
    )(pt, q3, kn3, slope_cols, *([ck] * len(k_specs)))

    v_specs = page_specs(VALUE_PAGES_PER_STEP)
    out = pl.pallas_call(
        functools.partial(_moba_decode_values_kernel, n_q=n_q),
        out_shape=jax.ShapeDtypeStruct((n_batch, n_q, d), F32),
        grid_spec=pltpu.PrefetchScalarGridSpec(
            num_scalar_prefetch=1,
            grid=(n_batch, n_pages // VALUE_PAGES_PER_STEP),
            in_specs=[pl.BlockSpec((1, chunks_per_value_step, rows, chunk),
                                   lambda b, c, pt_ref: (b, c, 0, 0)),
                      row_spec, row_spec, tok_spec, tok_spec] + v_specs,
            out_specs=tok_spec,
            scratch_shapes=[pltpu.VMEM((rows, d), F32)]),
        compiler_params=_params("arbitrary", "arbitrary"),
        name="moba_decode_values",
    )(pt, probs, p_own, denom, vn3, g3, *([cv] * len(v_specs)))
    return out.reshape(n_batch * n_q, d).astype(BF16)


def _cast_kernel(src_ref, dst_ref):
    dst_ref[...] = src_ref[...].astype(BF16)


def _cast_layer(w, layer, slab=128):
    _, r, c = w.shape
    return pl.pallas_call(
        _cast_kernel,
        out_shape=jax.ShapeDtypeStruct((1, r, c), BF16),
        grid=(r // slab,),
        in_specs=[pl.BlockSpec((None, slab, c), lambda i: (layer, i, 0))],
        out_specs=pl.BlockSpec((None, slab, c), lambda i: (0, i, 0)),
        compiler_params=_params("arbitrary"),
        name="cast_layer",
    )(w)


def _trunks(x_long, x_short, conv_prev_long, conv_prev_short, params, attend_long, attend_short):
    conv_w, ln_g, ln_b = params["conv_w"], params["ln_g"], params["ln_b"]
    n_l, t_l, d = x_long.shape
    n_s, t_s, _ = x_short.shape
    xl32, xs32 = x_long.reshape(n_l * t_l, d), x_short.reshape(n_s * t_s, d)
    xl16, xs16 = xl32.astype(BF16), xs32.astype(BF16)
    w16 = {("w_in_a", 0): _cast_layer(params["w_in_a"], 0)}

    def jobs(*names):
        return [(params[n] if params[n].ndim == 3 else params[n][None], l) for n, l in names]

    def keep(names, copies):
        w16.update(zip(names, copies))

    conv_l, conv_s = [], []
    ahead = [
        [("w_out_a", 0), ("w_in_a", 1), ("w_in_b", 1)],
        [("w_out_a", 1), ("w_kv", 0), ("w_in_b", 0), ("w_out_b", 0), ("w_out_b", 1)],
    ]
    for l in range(N_A):
        yl16, rows_l, copies, (ys16, rows_s) = _conv_proj(
            xl16, w16["w_in_a", l], 0, conv_w[l], conv_prev_long[l], t_l, jobs(*ahead[l]),
            short=(xs16, conv_prev_short[l], t_s))
        keep(ahead[l], copies)
        conv_l.append(rows_l)
        conv_s.append(rows_s)
        xl32, xl16, xs32, xs16 = _out_proj_ln(yl16, w16["w_out_a", l], 0, xl32, ln_g[l], ln_b[l],
                                              short=(ys16, xs32))
    kl, vl, _, (ks, vs) = _dual_proj(xl16, w16["w_kv", 0], 0, short=xs16)
    for l in range(N_A, DEPTH):
        ql, gl, _, (qs, gs) = _dual_proj(xl16, w16["w_in_b", l - N_A], 0, short=xs16)
        ol16 = attend_long(ql, gl, kl, vl)
        os16 = attend_short(qs, gs, ks, vs)
        xl32, xl16, xs32, xs16 = _out_proj_ln(ol16, w16["w_out_b", l - N_A], 0, xl32, ln_g[l], ln_b[l],
                                              short=(os16, xs32))
    heads_l, heads_s = (n_l, t_l, N_HEADS, HEAD_DIM), (n_s, t_s, N_HEADS, HEAD_DIM)
    return ((xl32.reshape(n_l, t_l, d), kl.reshape(heads_l), vl.reshape(heads_l), jnp.stack(conv_l)),
            (xs32.reshape(n_s, t_s, d), ks.reshape(heads_s), vs.reshape(heads_s), jnp.stack(conv_s)))


def kernel(x_prompt, x_sample, cache_k, cache_v, state_conv, page_table, w_in_a, conv_w, w_out_a,
           w_kv, w_in_b, w_out_b, ln_g, ln_b):
    params = dict(w_in_a=w_in_a, conv_w=conv_w, w_out_a=w_out_a, w_kv=w_kv, w_in_b=w_in_b,
                  w_out_b=w_out_b, ln_g=ln_g, ln_b=ln_b)
    n_prompt, seq, d = x_prompt.shape
    n_dec, dec_seq, _ = x_sample.shape
    zero_conv = jnp.zeros((N_A, n_prompt, CONV_W - 1, d), x_prompt.dtype)
    prompt, sample = _trunks(
        x_prompt, x_sample, zero_conv, state_conv, params,
        lambda q, g, k, v: _moba_prefill(q, k, v, g, n_prompt, seq),
        lambda q, g, k, v: _moba_decode(q, g, k, v, cache_k, cache_v, page_table, n_dec, dec_seq))
    y_prompt, k_prompt, v_prompt, conv_prompt = prompt
    y_sample, k_sample, v_sample, conv_sample = sample
    return (y_prompt, y_sample, k_prompt, v_prompt, conv_prompt, k_sample, v_sample, conv_sample)
```

```python
import functools
import math

import numpy as np
import jax
import jax.numpy as jnp
from jax import lax
from jax.experimental import pallas as pl
from jax.experimental.pallas import tpu as pltpu

N_HEADS = 32
HEAD_DIM = 128
CONV_W = 3
BLOCK = 256
TOP_K = 3
PAGE_SIZE = 128
DEPTH = 4
N_A = DEPTH // 2
ALPHA = (2.0 * DEPTH) ** 0.25
LN_EPS = 1e-5
SCALE = 1.0 / math.sqrt(HEAD_DIM)
LOG2_E = math.log2(math.e)

SUBLANES = 8
LANES = 128
VMEM_LIMIT_BYTES = 56 * 1024 * 1024
LN_VMEM_LIMIT_BYTES = 58 * 1024 * 1024
MASK_BIG = 2.0 ** 100
SCORE_PAGES_PER_STEP = 8
VALUE_PAGES_PER_STEP = 8
HEAD_GROUPS = N_HEADS // SUBLANES
ALIBI_PARTS = 4
HEADS_PER_STEP = 2

F32 = jnp.float32
BF16 = jnp.bfloat16


def _params(*semantics, vmem_limit=VMEM_LIMIT_BYTES):
    return pltpu.CompilerParams(dimension_semantics=semantics, vmem_limit_bytes=vmem_limit)


def _dot(a, b):
    return jnp.dot(a, b, preferred_element_type=F32)


def _dot_nt(a, b):
    return lax.dot_general(a, b, (((1,), (1,)), ((), ())), preferred_element_type=F32)


def _cast_jobs(jobs, n_steps, step_of):
    in_specs, out_specs, out_shapes, operands = [], [], [], []
    for w, layer in jobs:
        _, r, c = w.shape
        assert r % n_steps == 0 and (r // n_steps) % 16 == 0
        slab = r // n_steps
        in_specs.append(pl.BlockSpec(
            (None, slab, c), functools.partial(lambda *ids, layer: (layer, step_of(*ids), 0), layer=layer)))
        out_specs.append(pl.BlockSpec((None, slab, c), lambda *ids: (0, step_of(*ids), 0)))
        out_shapes.append(jax.ShapeDtypeStruct((1, r, c), BF16))
        operands.append(w)
    return in_specs, out_specs, out_shapes, operands


def _run_casts(src_refs, dst_refs):
    for src, dst in zip(src_refs, dst_refs):
        dst[...] = src[...].astype(BF16)


def _gated_conv(b_g, u, u_m1, u_m2, z, cw_ref):
    conv = cw_ref[0:1, :] * u_m2 + cw_ref[1:2, :] * u_m1 + cw_ref[2:3, :] * u
    return (b_g * conv) * (z * jax.nn.sigmoid(z))


def _conv_rows(x, w_refs, cw_ref, u_prev):
    wb_ref, wc_ref, wh_ref, wz_ref = w_refs
    u = _dot(x, wc_ref[...]) * _dot(x, wh_ref[...])
    u_m1, u_m2 = u_prev(u)
    return _gated_conv(_dot(x, wb_ref[...]), u, u_m1, u_m2, _dot(x, wz_ref[...]), cw_ref), u


def _conv_proj_kernel(x_ref, wb_ref, wc_ref, wh_ref, wz_ref, cw_ref, prev_ref, *rest,
                      tiles_per_seq, n_casts, short_len):
    n_short = 3 if short_len else 0
    short_in = rest[:n_short]
    cast_src = rest[n_short:n_short + n_casts]
    outs = rest[n_short + n_casts:]
    y_ref, tail_ref = outs[:2]
    short_out = outs[2:2 + (2 if short_len else 0)]
    cast_dst = outs[2 + len(short_out):2 + len(short_out) + n_casts]
    carry_ref = outs[-1]
    w_refs = (wb_ref, wc_ref, wh_ref, wz_ref)
    _run_casts(cast_src, cast_dst)
    m = pl.program_id(1)

    @pl.when(lax.rem(m, tiles_per_seq) == 0)
    def _():
        carry_ref[...] = prev_ref[0]

    def carried(u):
        row = lax.broadcasted_iota(jnp.int32, u.shape, 0)
        c_m2 = carry_ref[SUBLANES - 2:SUBLANES - 1, :]
        c_m1 = carry_ref[SUBLANES - 1:SUBLANES, :]
        u_m1 = jnp.where(row == 0, c_m1, pltpu.roll(u, 1, 0))
        u_m2 = jnp.where(row == 0, c_m2, jnp.where(row == 1, c_m1, pltpu.roll(u, 2, 0)))
        return u_m1, u_m2

    y, u = _conv_rows(x_ref[...], w_refs, cw_ref, carried)
    y_ref[...] = y.astype(y_ref.dtype)
    tail = u[u.shape[0] - SUBLANES:, :]
    carry_ref[...] = tail
    tail_ref[0] = tail

    if short_len:
        xs_ref, pm1_ref, pm2_ref = short_in
        ys_ref, us_ref = short_out

        @pl.when(m == 0)
        def _():
            def preceded(u_s):
                pos = lax.rem(lax.broadcasted_iota(jnp.int32, u_s.shape, 0), short_len)
                return (jnp.where(pos == 0, pm1_ref[...], pltpu.roll(u_s, 1, 0)),
                        jnp.where(pos < 2, pm2_ref[...], pltpu.roll(u_s, 2, 0)))

            y_s, u_s = _conv_rows(xs_ref[...], w_refs, cw_ref, preceded)
            ys_ref[...] = y_s.astype(ys_ref.dtype)
            us_ref[...] = u_s


def _conv_proj(x16, w_in16, layer, conv_w, conv_prev, seq_len, casts=(), short=None):
    m_rows, d = x16.shape
    n_seq = m_rows // seq_len
    tn = 256
    nj = d // tn
    w_specs = [pl.BlockSpec((None, d, tn),
                            functools.partial(lambda j, m, g: (layer, 0, g * nj + j), g=g))
               for g in range(4)]
    tm = 1024 if seq_len % 1024 == 0 else 512
    assert seq_len % tm == 0
    tiles_per_seq = seq_len // tm
    nm = m_rows // tm
    prev8 = jnp.pad(conv_prev.astype(F32), ((0, 0), (SUBLANES - (CONV_W - 1), 0), (0, 0)))
    c_in, c_out, c_shapes, c_ops = _cast_jobs(casts, nj * nm, lambda j, m: j * nm + m)
    s_in, s_out, s_shapes, s_ops, short_len = [], [], [], [], 0
    if short is not None:
        xs16, prev_s, short_len = short
        ms = xs16.shape[0]
        n_s = ms // short_len
        assert short_len >= CONV_W - 1 and ms % 16 == 0
        prev_s = prev_s.astype(F32)
        zeros = jnp.zeros((n_s, short_len - 1, d), F32)
        pm1 = jnp.concatenate([prev_s[:, 1:2], zeros], axis=1).reshape(ms, d)
        pm2 = jnp.concatenate([prev_s, zeros[:, 1:]], axis=1).reshape(ms, d)
        col = pl.BlockSpec((ms, tn), lambda j, m: (0, j))
        s_in = [pl.BlockSpec((ms, d), lambda j, m: (0, 0)), col, col]
        s_out = [col, col]
        s_shapes = [jax.ShapeDtypeStruct((ms, d), BF16), jax.ShapeDtypeStruct((ms, d), F32)]
        s_ops = [xs16, pm1, pm2]
    outs = pl.pallas_call(
        functools.partial(_conv_proj_kernel, tiles_per_seq=tiles_per_seq, n_casts=len(casts),
                          short_len=short_len),
        out_shape=(jax.ShapeDtypeStruct((m_rows, d), BF16),
                   jax.ShapeDtypeStruct((n_seq, SUBLANES, d), F32), *s_shapes, *c_shapes),
        grid=(nj, nm),
        in_specs=[pl.BlockSpec((tm, d), lambda j, m: (m, 0))] + w_specs + [
            pl.BlockSpec((CONV_W, tn), lambda j, m: (0, j)),
            pl.BlockSpec((1, SUBLANES, tn), lambda j, m: (m // tiles_per_seq, 0, j))] + s_in + c_in,
        out_specs=(pl.BlockSpec((tm, tn), lambda j, m: (m, j)),
                   pl.BlockSpec((1, SUBLANES, tn), lambda j, m: (m // tiles_per_seq, 0, j)),
                   *s_out, *c_out),
        scratch_shapes=[pltpu.VMEM((SUBLANES, tn), F32)],
        compiler_params=_params("arbitrary", "arbitrary"),
        name="conv_proj",
    )(x16, w_in16, w_in16, w_in16, w_in16, conv_w, prev8, *s_ops, *c_ops)
    y, tail = outs[:2]
    short_res = None
    if short is not None:
        y_s, u_s = outs[2:4]
        short_res = (y_s, u_s.reshape(n_s, short_len, d)[:, short_len - (CONV_W - 1):, :])
    return y, tail[:, SUBLANES - (CONV_W - 1):, :], list(outs[2 + len(s_shapes):]), short_res


def _fold_lanes(v):
    out = v[:, :LANES]
    for t in range(1, v.shape[1] // LANES):
        out = out + v[:, t * LANES:(t + 1) * LANES]
    return out


def _ln_collect(r, j, nj, d, r_ref, shift_ref, sum_ref, sq_ref, mean_ref, rstd_ref):
    r_ref[j] = r

    @pl.when(j == 0)
    def _():
        shift_ref[...] = jnp.broadcast_to(jnp.mean(r, axis=-1, keepdims=True), shift_ref.shape)
        sum_ref[...] = jnp.zeros_like(sum_ref)
        sq_ref[...] = jnp.zeros_like(sq_ref)

    dlt = r - shift_ref[:, 0:1]
    sum_ref[...] += _fold_lanes(dlt)
    sq_ref[...] += _fold_lanes(dlt * dlt)

    @pl.when(j == nj - 1)
    def _():
        e1 = jnp.sum(sum_ref[...], axis=-1, keepdims=True) / d
        e2 = jnp.sum(sq_ref[...], axis=-1, keepdims=True) / d
        var = jnp.maximum(e2 - e1 * e1, 0.0)
        mean_ref[...] = jnp.broadcast_to(shift_ref[:, 0:1] + e1, mean_ref.shape)
        rstd_ref[...] = jnp.broadcast_to(lax.rsqrt(var + LN_EPS), rstd_ref.shape)


def _ln_emit(j, r_ref, mean_ref, rstd_ref, g_ref, b_ref, xo_ref, xo16_ref):
    o = (r_ref[j] - mean_ref[:, 0:1]) * rstd_ref[:, 0:1] * g_ref[...] + b_ref[...]
    xo_ref[...] = o
    xo16_ref[...] = o.astype(xo16_ref.dtype)


def _out_proj_ln_kernel(y_ref, w_ref, xres_ref, g_ref, b_ref, *rest, nm, nj, d, has_short):
    n_s = 2 if has_short else 0
    short_in = rest[:n_s]
    xo_ref, xo16_ref = rest[n_s:n_s + 2]
    short_out = rest[n_s + 2:2 * n_s + 2]
    scratch = rest[2 * n_s + 2:]
    main_state, short_state = scratch[:6], scratch[6:]
    mm = pl.program_id(0)
    j = pl.program_id(1)

    def emit():
        _ln_emit(j, main_state[0], main_state[4], main_state[5], g_ref, b_ref, xo_ref, xo16_ref)

    def build(with_short):
        y = y_ref[...]
        tm = y.shape[0]
        if with_short:
            ys_ref, xres_s_ref = short_in
            y = jnp.concatenate([y, ys_ref[...]], axis=0)
        prod = _dot(y, w_ref[...])
        _ln_collect(ALPHA * xres_ref[...] + prod[:tm], j, nj, d, *main_state)
        if with_short:
            _ln_collect(ALPHA * xres_s_ref[...] + prod[tm:], j, nj, d, *short_state)

    def emit_and_build():
        emit()
        build(False)

    if has_short:
        @pl.when(mm == 1)
        def _():
            _ln_emit(j, short_state[0], short_state[4], short_state[5], g_ref, b_ref, *short_out)

    pl.when(mm == 0)(functools.partial(build, has_short))
    pl.when((mm > 0) & (mm < nm))(emit_and_build)
    pl.when(mm == nm)(emit)


def _out_proj_ln(y16, w16, layer, x_res, gain, bias, short=None):
    m_rows, k = y16.shape
    d = w16.shape[2]
    tm = min(1024, m_rows)
    tn = 512
    assert m_rows % tm == 0 and d % tn == 0
    nj = d // tn
    nm = m_rows // tm
    row_in = lambda mm: jnp.minimum(mm, nm - 1)
    col_in = lambda mm, j: jnp.where(mm < nm, j, nj - 1)
    out_idx = lambda mm, j: (jnp.maximum(mm - 1, 0), jnp.where(mm == 0, 0, j))

    def state(rows):
        stat = pltpu.VMEM((rows, LANES), F32)
        return [pltpu.VMEM((nj, rows, tn), F32), stat, stat, stat, stat, stat]

    s_in, s_out, s_shapes, s_ops, s_state = [], [], [], [], []
    if short is not None:
        ys16, x_res_s = short
        ms = ys16.shape[0]
        s_in = [pl.BlockSpec((ms, k), lambda mm, j: (0, 0)),
                pl.BlockSpec((ms, tn), lambda mm, j: (0, jnp.where(mm == 0, j, nj - 1)))]
        s_idx = lambda mm, j: (0, jnp.where(mm == 0, 0, jnp.where(mm == 1, j, nj - 1)))
        s_out = [pl.BlockSpec((ms, tn), s_idx), pl.BlockSpec((ms, tn), s_idx)]
        s_shapes = [jax.ShapeDtypeStruct((ms, d), F32), jax.ShapeDtypeStruct((ms, d), BF16)]
        s_ops = [ys16, x_res_s]
        s_state = state(ms)
        assert nm >= 1
    return pl.pallas_call(
        functools.partial(_out_proj_ln_kernel, nm=nm, nj=nj, d=d, has_short=short is not None),
        out_shape=(jax.ShapeDtypeStruct((m_rows, d), F32), jax.ShapeDtypeStruct((m_rows, d), BF16),
                   *s_shapes),
        grid=(nm + 1, nj),
        in_specs=[pl.BlockSpec((tm, k), lambda mm, j: (row_in(mm), 0)),
                  pl.BlockSpec((None, k, tn), lambda mm, j: (layer, 0, col_in(mm, j))),
                  pl.BlockSpec((tm, tn), lambda mm, j: (row_in(mm), col_in(mm, j))),
                  pl.BlockSpec((1, tn), lambda mm, j: (0, j)),
                  pl.BlockSpec((1, tn), lambda mm, j: (0, j))] + s_in,
        out_specs=(pl.BlockSpec((tm, tn), out_idx), pl.BlockSpec((tm, tn), out_idx), *s_out),
        scratch_shapes=state(tm) + s_state,
        compiler_params=_params("arbitrary", "arbitrary", vmem_limit=LN_VMEM_LIMIT_BYTES),
        name="out_proj_ln",
    )(y16, w16, x_res, gain.reshape(1, d), bias.reshape(1, d), *s_ops)


def _multi_proj_kernel(x_ref, *rest, n_w, has_short):
    n_s = 1 if has_short else 0
    w_refs = rest[:2 * n_w]
    outs = rest[2 * n_w + n_s:]
    main_out, short_out = outs[:2 * n_w], outs[2 * n_w:]

    def run(with_short):
        x = x_ref[...]
        tm = x.shape[0]
        if with_short:
            x = jnp.concatenate([x, rest[2 * n_w][...]], axis=0)
        for i, w_ref in enumerate(w_refs):
            prod = _dot(x, w_ref[...])
            main_out[i][...] = prod[:tm]
            if with_short:
                short_out[i][...] = prod[tm:]

    if has_short:
        first = pl.program_id(0) == 0
        pl.when(first)(functools.partial(run, True))
        pl.when(jnp.logical_not(first))(functools.partial(run, False))
    else:
        run(False)


def _multi_proj(x16, ws, short=None):
    m_rows, k = x16.shape
    n_w = len(ws)
    n_half = ws[0].shape[2] // 2
    tm = next(t for t in (1024, 512, m_rows) if m_rows % t == 0)
    tn = 512 // n_w
    nj = n_half // tn
    nm = m_rows // tm
    assert all(w.shape == ws[0].shape for w in ws) and n_half % tn == 0
    out = jax.ShapeDtypeStruct((m_rows, n_half), F32)
    w_specs, w_ops = [], []
    for w in ws:
        w_specs += [pl.BlockSpec((None, k, tn), lambda m, j: (0, 0, j)),
                    pl.BlockSpec((None, k, tn), lambda m, j: (0, 0, nj + j))]
        w_ops += [w, w]
    s_in, s_out, s_shapes, s_ops = [], [], [], []
    if short is not None:
        ms = short.shape[0]
        s_idx = lambda m, j: (0, jnp.where(m == 0, j, nj - 1))
        s_in = [pl.BlockSpec((ms, k), lambda m, j: (0, 0))]
        s_out = [pl.BlockSpec((ms, tn), s_idx)] * (2 * n_w)
        s_shapes = [jax.ShapeDtypeStruct((ms, n_half), F32)] * (2 * n_w)
        s_ops = [short]
    outs = pl.pallas_call(
        functools.partial(_multi_proj_kernel, n_w=n_w, has_short=short is not None),
        out_shape=(*([out] * (2 * n_w)), *s_shapes),
        grid=(nm, nj),
        in_specs=[pl.BlockSpec((tm, k), lambda m, j: (m, 0))] + w_specs + s_in,
        out_specs=(*([pl.BlockSpec((tm, tn), lambda m, j: (m, j))] * (2 * n_w)), *s_out),
        compiler_params=_params("arbitrary", "arbitrary"),
        name="multi_proj",
    )(x16, *w_ops, *s_ops)
    pairs = [(outs[2 * i], outs[2 * i + 1]) for i in range(n_w)]
    short_pairs = [(outs[2 * n_w + 2 * i], outs[2 * n_w + 2 * i + 1]) for i in range(n_w)] if short is not None else None
    return pairs, short_pairs


def _moba_prefill_kernel(q_ref, k_ref, v_ref, g_ref, kx_ref, qx_ref, o_ref, *scratch):
    seq = q_ref.shape[0]
    nb = seq // BLOCK
    per_head = 3 + nb
    r_i = lax.broadcasted_iota(jnp.int32, (BLOCK, BLOCK), 0)
    c_i = lax.broadcasted_iota(jnp.int32, (BLOCK, BLOCK), 1)
    causal = jnp.where(c_i <= r_i, 0.0, -jnp.inf)
    state = []
    for hh in range(HEADS_PER_STEP):
        qa_ref, ka_ref, vb_ref = scratch[hh * per_head:hh * per_head + 3]
        s_refs = scratch[hh * per_head + 3:(hh + 1) * per_head]
        cols = slice(hh * HEAD_DIM, (hh + 1) * HEAD_DIM)
        q16 = q_ref[:, cols].astype(BF16)
        k = k_ref[:, cols]
        kmean = jnp.mean(k.reshape(nb, BLOCK, HEAD_DIM), axis=1)
        gate_t = _dot_nt(kmean.astype(BF16), q16)
        blk = lax.broadcasted_iota(jnp.int32, (nb, seq), 0)
        q_blk = lax.broadcasted_iota(jnp.int32, (nb, seq), 1) // BLOCK
        rank = jnp.zeros((nb, seq), jnp.int32)
        for other in range(nb):
            g_o = gate_t[other:other + 1, :]
            beats = (g_o > gate_t) | ((g_o == gate_t) & (other < blk))
            rank += jnp.where((q_blk > other) & beats, 1, 0)
        dropped = ((blk < q_blk) & (rank >= TOP_K)).astype(F32)
        lane = lax.broadcasted_iota(jnp.int32, (nb, LANES), 1)
        spread = jnp.where(lane == lax.broadcasted_iota(jnp.int32, (nb, LANES), 0), -MASK_BIG, 0.0)
        sel_bias = lax.dot_general(dropped, spread, (((0,), (0,)), ((), ())),
                                   preferred_element_type=F32)
        qa_ref[:, :HEAD_DIM] = q16
        qa_ref[:, HEAD_DIM:] = (sel_bias + qx_ref[hh, 0:1, :]).astype(BF16)
        ka_ref[:, :HEAD_DIM] = k.astype(BF16)
        ka_ref[:, HEAD_DIM:] = kx_ref[...]
        vb_ref[:, :HEAD_DIM] = v_ref[:, cols].astype(BF16)
        ones_col = lax.broadcasted_iota(jnp.int32, (seq, HEAD_DIM), 1) == 0
        vb_ref[:, HEAD_DIM:] = jnp.where(ones_col, 1.0, 0.0).astype(BF16)
        state.append((qa_ref, ka_ref, vb_ref, s_refs, cols))
    for i in range(nb):
        for qa_ref, ka_ref, vb_ref, s_refs, cols in state:
            s_refs[i][...] = _dot_nt(qa_ref[i * BLOCK:, :], ka_ref[i * BLOCK:(i + 1) * BLOCK, :])
        for qa_ref, ka_ref, vb_ref, s_refs, cols in state:
            rows = slice(i * BLOCK, (i + 1) * BLOCK)
            n_keys = (i + 1) * BLOCK
            parts = [s_refs[n][(i - n) * BLOCK:(i - n + 1) * BLOCK, :] for n in range(i)]
            parts.append(s_refs[i][0:BLOCK, :] + causal)
            s = parts[0] if i == 0 else jnp.concatenate(parts, axis=1)
            m = jnp.max(s, axis=-1, keepdims=True)
            p = jnp.exp2((s - m) * (SCALE * LOG2_E))
            ol = _dot(p.astype(BF16), vb_ref[0:n_keys, :])
            o = ol[:, :HEAD_DIM] / ol[:, HEAD_DIM:HEAD_DIM + 1]
            gate = g_ref[rows, cols]
            o_ref[rows, cols] = (o * (gate * jax.nn.sigmoid(gate))).astype(o_ref.dtype)


def _alibi_slopes():
    return 2.0 ** (-8.0 * jnp.arange(1, N_HEADS + 1, dtype=F32) / N_HEADS)


def _moba_prefill(q, k, v, g, n_batch, seq):
    nb = seq // BLOCK
    assert nb <= SUBLANES and seq % BLOCK == 0
    pos = np.arange(seq)
    kx = np.zeros((seq, LANES), np.float32)
    kx[pos, pos // BLOCK] = 1.0
    kx[:, SUBLANES:SUBLANES + ALIBI_PARTS] = (pos % BLOCK)[:, None]
    kx[:, SUBLANES + ALIBI_PARTS:SUBLANES + 2 * ALIBI_PARTS] = ((pos // BLOCK) * BLOCK)[:, None]
    rest = _alibi_slopes() / SCALE
    pieces = []
    for _ in range(ALIBI_PARTS):
        piece = rest.astype(BF16).astype(F32)
        pieces.append(piece)
        rest = rest - piece
    pieces = jnp.stack(pieces, axis=1)
    qx = jnp.zeros((N_HEADS, LANES), F32)
    qx = qx.at[:, SUBLANES:SUBLANES + ALIBI_PARTS].set(pieces)
    qx = qx.at[:, SUBLANES + ALIBI_PARTS:SUBLANES + 2 * ALIBI_PARTS].set(pieces)
    qx = jnp.broadcast_to(qx[:, None, :], (N_HEADS, SUBLANES, LANES))
    pair_spec = pl.BlockSpec((seq, HEADS_PER_STEP * HEAD_DIM), lambda b, h: (b, h))
    wide = pltpu.VMEM((seq, 2 * HEAD_DIM), BF16)
    per_head = [wide, wide, wide] + [pltpu.VMEM(((nb - i) * BLOCK, BLOCK), F32) for i in range(nb)]
    return pl.pallas_call(
        _moba_prefill_kernel,
        out_shape=jax.ShapeDtypeStruct(q.shape, BF16),
        grid=(n_batch, N_HEADS // HEADS_PER_STEP),
        in_specs=[pair_spec, pair_spec, pair_spec, pair_spec,
                  pl.BlockSpec((seq, LANES), lambda b, h: (0, 0)),
                  pl.BlockSpec((HEADS_PER_STEP, SUBLANES, LANES), lambda b, h: (h, 0, 0))],
        out_specs=pair_spec,
        scratch_shapes=per_head * HEADS_PER_STEP,
        compiler_params=_params("arbitrary", "arbitrary"),
        name="moba_prefill",
    )(q, k, v, g, jnp.asarray(kx, BF16), qx)


def _block_diag_rows(x, n_q):
    rows = N_HEADS * n_q
    tiled = jnp.concatenate([x] * N_HEADS, axis=0)
    r_h = lax.broadcasted_iota(jnp.int32, (rows, x.shape[1]), 0) // n_q
    c_h = lax.broadcasted_iota(jnp.int32, (rows, x.shape[1]), 1) // HEAD_DIM
    return jnp.where(r_h == c_h, tiled, 0.0)


def _pad_rows(x, rows):
    return jnp.concatenate([x, jnp.zeros((rows - x.shape[0], x.shape[1]), x.dtype)], axis=0)


def _head_major(page_refs):
    n_pages = len(page_refs) // HEAD_GROUPS
    cols = []
    for h in range(N_HEADS):
        hg, sl = divmod(h, SUBLANES)
        rows = [page_refs[i * HEAD_GROUPS + hg].reshape(PAGE_SIZE * SUBLANES, HEAD_DIM)[
            pl.ds(sl, PAGE_SIZE, stride=SUBLANES), :] for i in range(n_pages)]
        cols.append(jnp.concatenate(rows, axis=0).astype(BF16))
    return jnp.concatenate(cols, axis=1)


def _moba_decode_scores_kernel(pt_ref, q_ref, knew_ref, slope_ref, *rest, n_q, past_len):
    n_in = SCORE_PAGES_PER_STEP * HEAD_GROUPS
    pages = rest[:n_in]
    p_ref, pown_ref, l_ref, s_ref, ksum_ref, qbd_ref, pick_ref = rest[n_in:]
    c = pl.program_id(1)
    n_chunks = pl.num_programs(1)
    chunk = SCORE_PAGES_PER_STEP * PAGE_SIZE
    blocks_per_chunk = chunk // BLOCK
    n_blocks = past_len // BLOCK
    rows = N_HEADS * n_q

    @pl.when(c == 0)
    def _():
        qbd_ref[...] = _block_diag_rows(q_ref[0], n_q).T.astype(BF16)

    pages_per_block = BLOCK // PAGE_SIZE
    for bi in range(blocks_per_chunk):
        for hg in range(HEAD_GROUPS):
            tot = None
            for i in range(bi * pages_per_block, (bi + 1) * pages_per_block):
                part = jnp.sum(pages[i * HEAD_GROUPS + hg][...], axis=0)
                tot = part if tot is None else tot + part
            row = ((c * blocks_per_chunk + bi) * HEAD_GROUPS + hg) * SUBLANES
            ksum_ref[pl.ds(pl.multiple_of(row, SUBLANES), SUBLANES), :] = tot
    s_ref[c] = _dot(_head_major(pages), qbd_ref[...])

    @pl.when(c == n_chunks - 1)
    def _():
        qbd = qbd_ref[...]
        ksum = jnp.concatenate([ksum_ref[pl.ds(h, n_blocks, stride=N_HEADS), :]
                                for h in range(N_HEADS)], axis=1)
        kmean = (ksum / BLOCK).astype(BF16)
        gate = _dot(kmean, qbd)
        idx = lax.broadcasted_iota(jnp.int32, gate.shape, 0)
        picked = jnp.zeros(gate.shape, jnp.bool_)
        for _ in range(TOP_K):
            best = jnp.max(gate, axis=0, keepdims=True)
            first = jnp.min(jnp.where(gate == best, idx, n_blocks), axis=0, keepdims=True)
            hit = idx == first
            picked = picked | hit
            gate = jnp.where(hit, -jnp.inf, gate)
        pick_ref[...] = picked.astype(F32)

        slope = slope_ref[0:1, :]
        q_i = lax.rem(lax.broadcasted_iota(jnp.int32, (1, rows), 1), n_q)
        t = (past_len + q_i).astype(F32)

        def mask_chunk(ci, m_run):
            pos = (ci * chunk + lax.broadcasted_iota(jnp.int32, (chunk, 1), 0)).astype(F32)
            s = s_ref[ci] * SCALE - slope * (t - pos)
            parts = []
            for bi in range(blocks_per_chunk):
                keep = pick_ref[pl.ds(ci * blocks_per_chunk + bi, 1), :]
                parts.append(jnp.where(keep > 0.5, s[bi * BLOCK:(bi + 1) * BLOCK], -jnp.inf))
            s = jnp.concatenate(parts, axis=0)
            s_ref[ci] = s
            return jnp.maximum(m_run, jnp.max(s, axis=0, keepdims=True))

        m_past = lax.fori_loop(0, n_chunks, mask_chunk, jnp.full((1, rows), -jnp.inf, F32))

        k_own = _pad_rows(knew_ref[0], LANES).astype(BF16)
        j_own = lax.broadcasted_iota(jnp.int32, (LANES, rows), 0)
        s_own = _dot(k_own, qbd) * SCALE - slope * (q_i - j_own).astype(F32)
        s_own = jnp.where(j_own <= q_i, s_own, -jnp.inf)
        m_all = jnp.maximum(m_past, jnp.max(s_own, axis=0, keepdims=True))
        p_own = jnp.exp(s_own - m_all)
        pown_ref[0] = p_own.T

        def exp_chunk(ci, l_run):
            p = jnp.exp(s_ref[ci] - m_all)
            p_ref[0, ci] = p.T.astype(p_ref.dtype)
            return l_run + jnp.sum(p, axis=0, keepdims=True)

        l_all = lax.fori_loop(0, n_chunks, exp_chunk, jnp.sum(p_own, axis=0, keepdims=True))
        l_ref[0] = jnp.broadcast_to(l_all, (LANES, rows)).T


def _moba_decode_values_kernel(pt_ref, p_ref, pown_ref, l_ref, vnew_ref, g_ref, *rest, n_q):
    n_in = VALUE_PAGES_PER_STEP * HEAD_GROUPS
    pages = rest[:n_in]
    o_ref, acc_ref = rest[n_in:]
    c = pl.program_id(1)
    pages_per_block = BLOCK // PAGE_SIZE
    blocks_per_chunk = SCORE_PAGES_PER_STEP // pages_per_block

    part = None
    for blk in range(VALUE_PAGES_PER_STEP // pages_per_block):
        slabs = pages[blk * pages_per_block * HEAD_GROUPS:(blk + 1) * pages_per_block * HEAD_GROUPS]
        sub, within = divmod(blk, blocks_per_chunk)
        term = _dot(p_ref[0, sub, :, within * BLOCK:(within + 1) * BLOCK], _head_major(slabs))
        part = term if part is None else part + term

    @pl.when(c == 0)
    def _():
        acc_ref[...] = part

    @pl.when(c > 0)
    def _():
        acc_ref[...] += part

    @pl.when(c == pl.num_programs(1) - 1)
    def _():
        v_own = _pad_rows(vnew_ref[0], LANES).astype(BF16)
        full = (acc_ref[...] + _dot(pown_ref[0].astype(BF16), v_own)) / l_ref[0][:, 0:1]
        r_h = lax.broadcasted_iota(jnp.int32, full.shape, 0) // n_q
        c_h = lax.broadcasted_iota(jnp.int32, full.shape, 1) // HEAD_DIM
        o = jnp.sum(jnp.where(r_h == c_h, full, 0.0).reshape(N_HEADS, n_q, full.shape[1]), axis=0)
        gate = g_ref[0]
        o_ref[0] = o * (gate * jax.nn.sigmoid(gate))


def _moba_decode(q, g, k_new, v_new, cache_k, cache_v, page_table, n_batch, n_q):
    d = q.shape[1]
    n_pages = page_table.shape[1]
    past_len = n_pages * PAGE_SIZE
    assert n_q == SUBLANES and past_len % BLOCK == 0 and n_pages % VALUE_PAGES_PER_STEP == 0
    assert VALUE_PAGES_PER_STEP % SCORE_PAGES_PER_STEP == 0
    n_chunks = n_pages // SCORE_PAGES_PER_STEP
    chunk = SCORE_PAGES_PER_STEP * PAGE_SIZE
    chunks_per_value_step = VALUE_PAGES_PER_STEP // SCORE_PAGES_PER_STEP
    n_blocks = past_len // BLOCK
    rows = N_HEADS * n_q
    pt = page_table.reshape(-1).astype(jnp.int32)
    ck = cache_k.reshape(cache_k.shape[0] * PAGE_SIZE, N_HEADS, HEAD_DIM)
    cv = cache_v.reshape(cache_v.shape[0] * PAGE_SIZE, N_HEADS, HEAD_DIM)
    q3, g3 = q.reshape(n_batch, n_q, d), g.reshape(n_batch, n_q, d)
    kn3, vn3 = k_new.reshape(n_batch, n_q, d), v_new.reshape(n_batch, n_q, d)
    slope_cols = jnp.broadcast_to(jnp.repeat(_alibi_slopes(), n_q)[None, :], (SUBLANES, rows))

    def page_specs(pages_per_step):
        def one(i, hg):
            return pl.BlockSpec(
                (PAGE_SIZE, SUBLANES, HEAD_DIM),
                lambda b, c, pt_ref: (pt_ref[b * n_pages + c * pages_per_step + i], hg, 0))
        return [one(i, hg) for i in range(pages_per_step) for hg in range(HEAD_GROUPS)]

    tok_spec = pl.BlockSpec((1, n_q, d), lambda b, c, pt_ref: (b, 0, 0))
    row_spec = pl.BlockSpec((1, rows, LANES), lambda b, c, pt_ref: (b, 0, 0))
    k_specs = page_specs(SCORE_PAGES_PER_STEP)
    probs, p_own, denom = pl.pallas_call(
        functools.partial(_moba_decode_scores_kernel, n_q=n_q, past_len=past_len),
        out_shape=(jax.ShapeDtypeStruct((n_batch, n_chunks, rows, chunk), BF16),
                   jax.ShapeDtypeStruct((n_batch, rows, LANES), F32),
                   jax.ShapeDtypeStruct((n_batch, rows, LANES), F32)),
        grid_spec=pltpu.PrefetchScalarGridSpec(
            num_scalar_prefetch=1,
            grid=(n_batch, n_chunks),
            in_specs=[tok_spec, tok_spec,
                      pl.BlockSpec((SUBLANES, rows), lambda b, c, pt_ref: (0, 0))] + k_specs,
            out_specs=(pl.BlockSpec((1, n_chunks, rows, chunk), lambda b, c, pt_ref: (b, 0, 0, 0)),
                       row_spec, row_spec),
            scratch_shapes=[pltpu.VMEM((n_chunks, chunk, rows), F32),
                            pltpu.VMEM((n_blocks * N_HEADS, HEAD_DIM), F32),
                            pltpu.VMEM((d, rows), BF16),
                            pltpu.VMEM((n_blocks, rows), F32)]),
        compiler_params=_params("arbitrary", "arbitrary"),
        name="moba_decode_scores",
    )(pt, q3, kn3, slope_cols, *([ck] * len(k_specs)))

    v_specs = page_specs(VALUE_PAGES_PER_STEP)
    out = pl.pallas_call(
        functools.partial(_moba_decode_values_kernel, n_q=n_q),
        out_shape=jax.ShapeDtypeStruct((n_batch, n_q, d), F32),
        grid_spec=pltpu.PrefetchScalarGridSpec(
            num_scalar_prefetch=1,
            grid=(n_batch, n_pages // VALUE_PAGES_PER_STEP),
            in_specs=[pl.BlockSpec((1, chunks_per_value_step, rows, chunk),
                                   lambda b, c, pt_ref: (b, c, 0, 0)),
                      row_spec, row_spec, tok_spec, tok_spec] + v_specs,
            out_specs=tok_spec,
            scratch_shapes=[pltpu.VMEM((rows, d), F32)]),
        compiler_params=_params("arbitrary", "arbitrary"),
        name="moba_decode_values",
    )(pt, probs, p_own, denom, vn3, g3, *([cv] * len(v_specs)))
    return out.reshape(n_batch * n_q, d).astype(BF16)


def _cast_kernel(src_ref, dst_ref):
    dst_ref[...] = src_ref[...].astype(BF16)


def _cast_layer(w, layer, slab=128):
    _, r, c = w.shape
    return pl.pallas_call(
        _cast_kernel,
        out_shape=jax.ShapeDtypeStruct((1, r, c), BF16),
        grid=(r // slab,),
        in_specs=[pl.BlockSpec((None, slab, c), lambda i: (layer, i, 0))],
        out_specs=pl.BlockSpec((None, slab, c), lambda i: (0, i, 0)),
        compiler_params=_params("arbitrary"),
        name="cast_layer",
    )(w)


def _trunks(x_long, x_short, conv_prev_long, conv_prev_short, params, attend_long, attend_short):
    conv_w, ln_g, ln_b = params["conv_w"], params["ln_g"], params["ln_b"]
    n_l, t_l, d = x_long.shape
    n_s, t_s, _ = x_short.shape
    xl32, xs32 = x_long.reshape(n_l * t_l, d), x_short.reshape(n_s * t_s, d)
    xl16, xs16 = xl32.astype(BF16), xs32.astype(BF16)
    w16 = {("w_in_a", 0): _cast_layer(params["w_in_a"], 0)}

    def jobs(*names):
        return [(params[n] if params[n].ndim == 3 else params[n][None], l) for n, l in names]

    def keep(names, copies):
        w16.update(zip(names, copies))

    conv_l, conv_s = [], []
    ahead = [
        [("w_out_a", 0), ("w_in_a", 1), ("w_in_b", 1)],
        [("w_out_a", 1), ("w_kv", 0), ("w_in_b", 0), ("w_out_b", 0), ("w_out_b", 1)],
    ]
    for l in range(N_A):
        yl16, rows_l, copies, (ys16, rows_s) = _conv_proj(
            xl16, w16["w_in_a", l], 0, conv_w[l], conv_prev_long[l], t_l, jobs(*ahead[l]),
            short=(xs16, conv_prev_short[l], t_s))
        keep(ahead[l], copies)
        conv_l.append(rows_l)
        conv_s.append(rows_s)
        xl32, xl16, xs32, xs16 = _out_proj_ln(yl16, w16["w_out_a", l], 0, xl32, ln_g[l], ln_b[l],
                                              short=(ys16, xs32))
    ((kl, vl), (ql, gl)), ((ks, vs), (qs, gs)) = _multi_proj(
        xl16, [w16["w_kv", 0], w16["w_in_b", 0]], short=xs16)
    for l in range(N_A, DEPTH):
        if l > N_A:
            ((ql, gl),), ((qs, gs),) = _multi_proj(xl16, [w16["w_in_b", l - N_A]], short=xs16)
        ol16 = attend_long(ql, gl, kl, vl)
        os16 = attend_short(qs, gs, ks, vs)
        xl32, xl16, xs32, xs16 = _out_proj_ln(ol16, w16["w_out_b", l - N_A], 0, xl32, ln_g[l], ln_b[l],
                                              short=(os16, xs32))
    heads_l, heads_s = (n_l, t_l, N_HEADS, HEAD_DIM), (n_s, t_s, N_HEADS, HEAD_DIM)
    return ((xl32.reshape(n_l, t_l, d), kl.reshape(heads_l), vl.reshape(heads_l), jnp.stack(conv_l)),
            (xs32.reshape(n_s, t_s, d), ks.reshape(heads_s), vs.reshape(heads_s), jnp.stack(conv_s)))


def kernel(x_prompt, x_sample, cache_k, cache_v, state_conv, page_table, w_in_a, conv_w, w_out_a,
           w_kv, w_in_b, w_out_b, ln_g, ln_b):
    params = dict(w_in_a=w_in_a, conv_w=conv_w, w_out_a=w_out_a, w_kv=w_kv, w_in_b=w_in_b,
                  w_out_b=w_out_b, ln_g=ln_g, ln_b=ln_b)
    n_prompt, seq, d = x_prompt.shape
    n_dec, dec_seq, _ = x_sample.shape
    zero_conv = jnp.zeros((N_A, n_prompt, CONV_W - 1, d), x_prompt.dtype)
    prompt, sample = _trunks(
        x_prompt, x_sample, zero_conv, state_conv, params,
        lambda q, g, k, v: _moba_prefill(q, k, v, g, n_prompt, seq),
        lambda q, g, k, v: _moba_decode(q, g, k, v, cache_k, cache_v, page_table, n_dec, dec_seq))
    y_prompt, k_prompt, v_prompt, conv_prompt = prompt
    y_sample, k_sample, v_sample, conv_sample = sample
    return (y_prompt, y_sample, k_prompt, v_prompt, conv_prompt, k_sample, v_sample, conv_sample)
```
